```python
import jax, jax.numpy as jnp
from jax import lax
import numpy as np

D_MODEL = 1024
BATCH = 16
SEQ = 4096
DEPTH = 2

CTX_LEN = 256
GRID_W = 64
HEAD_DIM = 64
N_HEADS = D_MODEL // HEAD_DIM
A_GROUPS = N_HEADS // 4
B_HEADS = (N_HEADS - A_GROUPS) // 2
C_HEADS = N_HEADS - A_GROUPS - B_HEADS
A_WIDTH = A_GROUPS * HEAD_DIM
B_WIDTH = B_HEADS * HEAD_DIM
C_WIDTH = C_HEADS * HEAD_DIM
MIX_WIDTH = A_WIDTH + B_WIDTH + C_WIDTH
A_CHUNK = 128
WIN_ROWS = 8
WIN_COLS = 16
C_CHUNK = 64
C_CONV = 3
D_FF = ((8 * D_MODEL + 3 * 256 - 1) // (3 * 256)) * 256
N_MOD = 6
ROPE_BASE = 10000.0
EPS = 1e-6
SPLITS = (A_WIDTH, A_WIDTH,
          B_WIDTH, B_WIDTH, B_WIDTH,
          C_WIDTH, C_WIDTH, C_WIDTH, C_WIDTH,
          4 * C_HEADS)
IN_COLS = sum(SPLITS)

kernel_name = 'hybrid_gmlp_natten_mlstm_dit_block'


def rms_norm(x, g):
    xf = x.astype(jnp.float32)
    y = xf * lax.rsqrt(jnp.mean(xf * xf, axis=-1, keepdims=True) + EPS)
    return (y * g.astype(jnp.float32)).astype(x.dtype)


def layer_norm(x, g, b):
    xf = x.astype(jnp.float32)
    mu = jnp.mean(xf, axis=-1, keepdims=True)
    var = jnp.mean(jnp.square(xf - mu), axis=-1, keepdims=True)
    y = (xf - mu) * lax.rsqrt(var + EPS)
    return (y * g.astype(jnp.float32) + b.astype(jnp.float32)).astype(x.dtype)


def modulate(x, g, shift, scale):
    return rms_norm(x, g) * (1 + scale) + shift


def split_cols(z):
    return jnp.split(z, np.cumsum(SPLITS)[:-1].tolist(), axis=-1)


def to_heads(z, n_heads):
    b_, n, _ = z.shape
    return z.reshape(b_, n, n_heads, HEAD_DIM).transpose(0, 2, 1, 3)


def from_heads(h):
    b_, h_, n, dh = h.shape
    return h.transpose(0, 2, 1, 3).reshape(b_, n, h_ * dh)


def swiglu(h, w_gate, w_up, w_down):
    return (jax.nn.silu(h @ w_gate) * (h @ w_up)) @ w_down


def chunk_token_mlp(z_u, z_v, ln_g, ln_b, ws, bs):
    b_, n, _ = z_u.shape
    u = jax.nn.gelu(z_u).reshape(b_, n, A_GROUPS, HEAD_DIM)
    v = layer_norm(jax.nn.gelu(z_v).reshape(b_, n, A_GROUPS, HEAD_DIM), ln_g, ln_b)
    v = v.reshape(b_, n // A_CHUNK, A_CHUNK, A_GROUPS, HEAD_DIM)
    s = jnp.einsum('gpq,bcqgd->bcpgd', ws, v) + bs.T[:, :, None]
    return (u * s.reshape(b_, n, A_GROUPS, HEAD_DIM)).reshape(b_, n, A_WIDTH)


def neighbourhood_attention(q, k, v, k_ctx, v_ctx, rpb):
    b_, h_, n, dh = q.shape
    rows = n // GRID_W
    kr = min(WIN_ROWS, rows)
    win = kr * WIN_COLS
    scale = dh ** -0.5
    col = jnp.arange(GRID_W)
    key_col = jnp.clip(col - WIN_COLS // 2, 0, GRID_W - WIN_COLS)[:, None] + jnp.arange(WIN_COLS)
    dc = key_col - col[:, None]
    q_rows = q.reshape(b_, h_, rows, GRID_W, dh).transpose(2, 0, 1, 3, 4)

    def row_block(args):
        r, q_blk = args
        key_row = jnp.clip(r - kr // 2, 0, rows - kr) + jnp.arange(kr)
        idx = (key_row[None, :, None] * GRID_W + key_col[:, None, :]).reshape(GRID_W, win)
        k_win = jnp.take(k, idx, axis=2)
        v_win = jnp.take(v, idx, axis=2)
        bias = rpb[:, (key_row - r)[None, :, None] + WIN_ROWS - 1,
                   dc[:, None, :] + WIN_COLS - 1].reshape(h_, GRID_W, win)
        s_win = jnp.einsum('bhqd,bhqkd->bhqk', q_blk, k_win).astype(jnp.float32) * scale + bias.astype(jnp.float32)
        s_ctx = jnp.einsum('bhqd,bhkd->bhqk', q_blk, k_ctx).astype(jnp.float32) * scale
        p = jax.nn.softmax(jnp.concatenate([s_win, s_ctx], axis=-1), axis=-1).astype(v.dtype)
        return (jnp.einsum('bhqk,bhqkd->bhqd', p[..., :win], v_win)
                + jnp.einsum('bhqk,bhkd->bhqd', p[..., win:], v_ctx))

    out = lax.map(row_block, (jnp.arange(rows), q_rows))
    return out.transpose(1, 2, 0, 3, 4).reshape(b_, h_, n, dh)


def context_attention(q, k, v):
    s = jnp.einsum('bhqd,bhkd->bhqk', q, k).astype(jnp.float32) * (q.shape[-1] ** -0.5)
    p = jax.nn.softmax(s, axis=-1).astype(v.dtype)
    return jnp.einsum('bhqk,bhkd->bhqd', p, v)


def axial_rope(x):
    n = x.shape[1]
    t = jnp.arange(n)
    half = HEAD_DIM // 2
    quarter = half // 2
    inv_freq = ROPE_BASE ** (-jnp.arange(quarter, dtype=jnp.float32) / quarter)

    def rotate(xa, pos):
        ang = pos.astype(jnp.float32)[:, None] * inv_freq
        cos = jnp.cos(ang)[None, :, None, :]
        sin = jnp.sin(ang)[None, :, None, :]
        x1 = xa[..., :quarter].astype(jnp.float32)
        x2 = xa[..., quarter:].astype(jnp.float32)
        return jnp.concatenate([x1 * cos - x2 * sin, x1 * sin + x2 * cos], axis=-1)

    out = jnp.concatenate([rotate(x[..., :half], t // GRID_W), rotate(x[..., half:], t % GRID_W)], axis=-1)
    return out.astype(x.dtype)


def centred_dwconv(x, w, b):
    pad = (C_CONV - 1) // 2
    y = lax.conv_general_dilated(x, w[:, None, :].astype(x.dtype), (1,), [(pad, C_CONV - 1 - pad)],
                                 dimension_numbers=('NWC', 'WIO', 'NWC'), feature_group_count=x.shape[-1])
    return y + b


def mlstm_inputs(zq, zk, zv, zg, conv_w, conv_b, gate_b, with_rope):
    b_, n, _ = zq.shape
    qk = jax.nn.silu(centred_dwconv(jnp.concatenate([zq, zk], axis=-1), conv_w, conv_b))
    q = qk[..., :C_WIDTH].reshape(b_, n, C_HEADS, HEAD_DIM)
    k = qk[..., C_WIDTH:].reshape(b_, n, C_HEADS, HEAD_DIM)
    if with_rope:
        q, k = axial_rope(q), axial_rope(k)
    heads = lambda a: a.transpose(0, 2, 1, 3).astype(jnp.float32)
    g = (zg.reshape(b_, n, 4, C_HEADS).astype(jnp.float32) + gate_b.astype(jnp.float32)).transpose(2, 0, 3, 1)
    fwd = (g[0], jax.nn.log_sigmoid(g[1]))
    bwd = (g[2], jax.nn.log_sigmoid(g[3]))
    v = zv.reshape(b_, n, C_HEADS, HEAD_DIM)
    return heads(q), heads(k) * (HEAD_DIM ** -0.5), heads(v), fwd, bwd


def mlstm_chunk_scan(q, k, v, log_i, log_f, state):
    b_, h_, n, dh = q.shape
    nc = n // C_CHUNK

    def chunks(a):
        return jnp.moveaxis(a.reshape(a.shape[:2] + (nc, C_CHUNK) + a.shape[3:]), 2, 0)

    scan_order = jnp.tril(jnp.ones((C_CHUNK, C_CHUNK), dtype=bool))

    def step(carry, inp):
        c_st, n_st, m_st = carry
        qb, kb, vb, ib, fb = inp
        fcum = jnp.cumsum(fb, axis=-1)
        log_inter = fcum + m_st[..., None]
        log_intra = jnp.where(scan_order, fcum[..., :, None] - fcum[..., None, :] + ib[..., None, :], -jnp.inf)
        m_t = jnp.maximum(log_inter, jnp.max(log_intra, axis=-1))
        w_inter = jnp.exp(log_inter - m_t)
        s = jnp.einsum('bhtd,bhsd->bhts', qb, kb) * jnp.exp(log_intra - m_t[..., None])
        num = w_inter[..., None] * jnp.einsum('bhtk,bhkv->bhtv', qb, c_st) + jnp.einsum('bhts,bhsv->bhtv', s, vb)
        den = w_inter * jnp.einsum('bhtk,bhk->bht', qb, n_st) + jnp.sum(s, axis=-1)
        h = num / jnp.maximum(jnp.abs(den), jnp.exp(-m_t))[..., None]
        log_w = fcum[..., -1:] - fcum + ib
        m_new = jnp.maximum(fcum[..., -1] + m_st, jnp.max(log_w, axis=-1))
        a_prev = jnp.exp(fcum[..., -1] + m_st - m_new)
        a_s = jnp.exp(log_w - m_new[..., None])
        c_new = a_prev[..., None, None] * c_st + jnp.einsum('bhs,bhsk,bhsv->bhkv', a_s, kb, vb)
        n_new = a_prev[..., None] * n_st + jnp.einsum('bhs,bhsk->bhk', a_s, kb)
        return (c_new, n_new, m_new), h

    state, hs = lax.scan(step, state, tuple(chunks(a) for a in (q, k, v, log_i, log_f)))
    return jnp.moveaxis(hs, 0, 2).reshape(b_, h_, n, dh), state


def mlstm_final_state(k, v, log_i, log_f):
    fcum = jnp.cumsum(log_f, axis=-1)
    log_w = fcum[..., -1:] - fcum + log_i
    m = jnp.maximum(fcum[..., -1], jnp.max(log_w, axis=-1))
    a = jnp.exp(log_w - m[..., None])
    return (jnp.einsum('bhs,bhsk,bhsv->bhkv', a, k, v), jnp.einsum('bhs,bhsk->bhk', a, k), m)


def mlstm_bidirectional(lat, ctx, ctx_out):
    q, k, v, (i_f, f_f), (i_b, f_b) = lat
    qc, kc, vc, (ic_f, fc_f), (ic_b, fc_b) = ctx
    rev = lambda a: jnp.flip(a, axis=2)
    if ctx_out:
        b_, h_, _, dh = kc.shape
        zero = (jnp.zeros((b_, h_, dh, dh), jnp.float32), jnp.zeros((b_, h_, dh), jnp.float32),
                jnp.zeros((b_, h_), jnp.float32))
        hc_f, st_f = mlstm_chunk_scan(qc, kc, vc, ic_f, fc_f, zero)
        hc_b, st_b = mlstm_chunk_scan(rev(qc), rev(kc), rev(vc), rev(ic_b), rev(fc_b), zero)
        h_ctx = hc_f + rev(hc_b)
    else:
        st_f = mlstm_final_state(kc, vc, ic_f, fc_f)
        st_b = mlstm_final_state(rev(kc), rev(vc), rev(ic_b), rev(fc_b))
        h_ctx = None
    h_f, _ = mlstm_chunk_scan(q, k, v, i_f, f_f, st_f)
    h_b, _ = mlstm_chunk_scan(rev(q), rev(k), rev(v), rev(i_b), rev(f_b), st_b)
    return h_f + rev(h_b), h_ctx


def hybrid_layer(x, xc, mod, mod_c, norm1_g, w_in, a_ln_g, a_ln_b, a_ws, a_bs, b_rpb,
                 c_conv_w, c_conv_b, c_gate_b, w_out, norm2_g, w_gate, w_up, w_down, ctx_out):
    sh1, sc1, g1, sh2, sc2, g2 = [mod[:, i] for i in range(N_MOD)]
    csh1, csc1, cg1, csh2, csc2, cg2 = [mod_c[i] for i in range(N_MOD)]
    z = modulate(x, norm1_g, sh1, sc1) @ w_in
    zc = modulate(xc, norm1_g, csh1, csc1) @ w_in
    au, av, bq, bk, bv, cq, ck, cv, co, cg = split_cols(z)
    au_c, av_c, bq_c, bk_c, bv_c, cq_c, ck_c, cv_c, co_c, cg_c = split_cols(zc)

    y_a = chunk_token_mlp(au, av, a_ln_g, a_ln_b, a_ws, a_bs)
    k_ctx, v_ctx = to_heads(bk_c, B_HEADS), to_heads(bv_c, B_HEADS)
    y_b = from_heads(neighbourhood_attention(to_heads(bq, B_HEADS), to_heads(bk, B_HEADS),
                                             to_heads(bv, B_HEADS), k_ctx, v_ctx, b_rpb))
    lat_in = mlstm_inputs(cq, ck, cv, cg, c_conv_w, c_conv_b, c_gate_b, True)
    ctx_in = mlstm_inputs(cq_c, ck_c, cv_c, cg_c, c_conv_w, c_conv_b, c_gate_b, False)
    h_lat, h_ctx = mlstm_bidirectional(lat_in, ctx_in, ctx_out)
    y_c = from_heads(h_lat).astype(x.dtype) * jax.nn.sigmoid(co)

    x = x + g1 * (jnp.concatenate([y_a, y_b, y_c], axis=-1) @ w_out)
    x = x + g2 * swiglu(modulate(x, norm2_g, sh2, sc2), w_gate, w_up, w_down)

    if ctx_out:
        y_ac = chunk_token_mlp(au_c, av_c, a_ln_g, a_ln_b, a_ws, a_bs)
        y_bc = from_heads(context_attention(to_heads(bq_c, B_HEADS), k_ctx, v_ctx))
        y_cc = from_heads(h_ctx).astype(xc.dtype) * jax.nn.sigmoid(co_c)
        xc = xc + cg1 * (jnp.concatenate([y_ac, y_bc, y_cc], axis=-1) @ w_out)
        xc = xc + cg2 * swiglu(modulate(xc, norm2_g, csh2, csc2), w_gate, w_up, w_down)
    return x, xc


def setup_inputs(seed: int = 0) -> dict:
    key = jax.random.key(seed)
    ks = jax.random.split(key, 24)
    f32 = jnp.float32
    nrm = lambda k, shape, s: jax.random.normal(k, shape, f32) * s
    gate_i = nrm(ks[14], (DEPTH, 2, C_HEADS), 0.1)
    gate_f = jnp.linspace(3.0, 6.0, C_HEADS, dtype=f32) + nrm(ks[15], (DEPTH, 2, C_HEADS), 0.1)
    c_gate_b = jnp.stack([gate_i[:, 0], gate_f[:, 0], gate_i[:, 1], gate_f[:, 1]], axis=1)
    return {
        'x': nrm(ks[0], (BATCH, SEQ, D_MODEL), 1.0),
        'c': nrm(ks[1], (BATCH, D_MODEL), 1.0),
        'ctx': nrm(ks[2], (BATCH, CTX_LEN, D_MODEL), 1.0),
        'c_ctx': nrm(ks[3], (D_MODEL,), 1.0),
        'w_mod': nrm(ks[4], (DEPTH, D_MODEL, N_MOD * D_MODEL), 0.5 * D_MODEL ** -0.5),
        'b_mod': nrm(ks[5], (DEPTH, N_MOD * D_MODEL), 0.02),
        'norm1_g': 1.0 + nrm(ks[6], (DEPTH, D_MODEL), 0.02),
        'w_in': nrm(ks[7], (DEPTH, D_MODEL, IN_COLS), D_MODEL ** -0.5),
        'a_ln_g': 1.0 + nrm(ks[8], (DEPTH, A_GROUPS, HEAD_DIM), 0.02),
        'a_ln_b': nrm(ks[9], (DEPTH, A_GROUPS, HEAD_DIM), 0.02),
        'a_ws': nrm(ks[10], (DEPTH, A_GROUPS, A_CHUNK, A_CHUNK), A_CHUNK ** -0.5),
        'a_bs': 1.0 + nrm(ks[11], (DEPTH, A_GROUPS, A_CHUNK), 0.02),
        'b_rpb': nrm(ks[12], (DEPTH, B_HEADS, 2 * WIN_ROWS - 1, 2 * WIN_COLS - 1), 0.02),
        'c_conv_w': nrm(ks[13], (DEPTH, C_CONV, 2 * C_WIDTH), C_CONV ** -0.5),
        'c_conv_b': nrm(ks[16], (DEPTH, 2 * C_WIDTH), 0.02),
        'c_gate_b': c_gate_b,
        'w_out': nrm(ks[17], (DEPTH, MIX_WIDTH, D_MODEL), MIX_WIDTH ** -0.5),
        'norm2_g': 1.0 + nrm(ks[18], (DEPTH, D_MODEL), 0.02),
        'w_gate': nrm(ks[19], (DEPTH, D_MODEL, D_FF), D_MODEL ** -0.5),
        'w_up': nrm(ks[20], (DEPTH, D_MODEL, D_FF), D_MODEL ** -0.5),
        'w_down': nrm(ks[21], (DEPTH, D_FF, D_MODEL), D_FF ** -0.5),
        'final_g': 1.0 + nrm(ks[22], (D_MODEL,), 0.02),
    }


def reference(x, c, ctx, c_ctx, w_mod, b_mod, norm1_g, w_in, a_ln_g, a_ln_b, a_ws, a_bs, b_rpb,
              c_conv_w, c_conv_b, c_gate_b, w_out, norm2_g, w_gate, w_up, w_down, final_g):
    b_, _, d = x.shape
    xc = ctx
    for l in range(DEPTH):
        mod = (jax.nn.silu(c) @ w_mod[l] + b_mod[l]).reshape(b_, N_MOD, 1, d)
        mod_c = (jax.nn.silu(c_ctx) @ w_mod[l] + b_mod[l]).reshape(N_MOD, d)
        x, xc = hybrid_layer(x, xc, mod, mod_c, norm1_g[l], w_in[l], a_ln_g[l], a_ln_b[l], a_ws[l], a_bs[l],
                             b_rpb[l], c_conv_w[l], c_conv_b[l], c_gate_b[l], w_out[l], norm2_g[l],
                             w_gate[l], w_up[l], w_down[l], l < DEPTH - 1)
    return rms_norm(x, final_g)
```

```python
import functools

import numpy as np
import jax
import jax.numpy as jnp
from jax import lax
from jax.experimental import pallas as pl
from jax.experimental.pallas import tpu as pltpu

F32 = jnp.float32
BF16 = jnp.bfloat16

LANES = 128
MXU_N = 256
HEAD_DIM = 64
GRID_W = 64
A_GROUPS = 4
A_CHUNK = 128
B_HEADS = 6
C_HEADS = 6
WIN_ROWS = 8
WIN_COLS = 16
ROPE_BASE = 10000.0
EPS = 1e-6
N_MOD = 6
NEG = -1e30

T_AU, T_AV, T_BQ, T_BK, T_BV, T_CQ, T_CK, T_CV, T_CO, T_CG = 0, 2, 4, 7, 10, 13, 16, 19, 22, 25
N_TILES = 26
N_PAIRS = 3
N_GATES = 4 * C_HEADS
Q_ROWS = 4
K_ROWS = Q_ROWS + WIN_ROWS
C_LEN = 256

VMEM_LIMIT = 56 * 1024 * 1024


def _cparams(sem):
    return pltpu.CompilerParams(dimension_semantics=sem, vmem_limit_bytes=VMEM_LIMIT)


def _dot(a, b):
    return jnp.dot(a, b, preferred_element_type=F32)


def _dot_nt(a, b):
    return lax.dot_general(a, b, (((1,), (1,)), ((), ())), preferred_element_type=F32)


def _dot_tn(a, b):
    return lax.dot_general(a, b, (((0,), (0,)), ((), ())), preferred_element_type=F32)


def _split3(x):
    h = x.astype(BF16)
    r = x - h.astype(F32)
    m = r.astype(BF16)
    l = (r - m.astype(F32)).astype(BF16)
    return h, m, l


def _dot3_r(x, sel):
    h, m, l = _split3(x)
    return _dot(h, sel) + _dot(m, sel) + _dot(l, sel)


def _dot3_l(sel, x):
    h, m, l = _split3(x)
    return _dot(sel, h) + _dot(sel, m) + _dot(sel, l)


def _silu(x):
    return x * (1.0 / (1.0 + jnp.exp(-x)))


def _sigmoid(x):
    return 1.0 / (1.0 + jnp.exp(-x))


def _gelu_tanh(x):
    return 0.5 * x * (1.0 + jnp.tanh(np.sqrt(2.0 / np.pi).astype(np.float32) * (x + 0.044715 * (x * x * x))))


def _log_sigmoid(x):
    return jnp.minimum(x, 0.0) - jnp.log1p(jnp.exp(-jnp.abs(x)))


def _rms_modulate(x, g, shift, scale):
    ms = jnp.mean(x * x, axis=-1, keepdims=True)
    return (x * lax.rsqrt(ms + EPS) * g) * (1.0 + scale) + shift


def _mod_kernel(c_ref, w_ref, b_ref, o_ref):
    o_ref[0] = _dot(_silu(c_ref[...]).astype(BF16), w_ref[0]) + b_ref[0, 0]


def _mod_call(cs, w_mod, b_mod):
    depth, d, nd = w_mod.shape
    rows = cs.shape[0]
    return pl.pallas_call(
        _mod_kernel,
        grid=(depth, nd // d),
        in_specs=[pl.BlockSpec((rows, d), lambda l, n: (0, 0)),
                  pl.BlockSpec((1, d, d), lambda l, n: (l, 0, n)),
                  pl.BlockSpec((1, 1, 1, d), lambda l, n: (l, n, 0, 0))],
        out_specs=pl.BlockSpec((1, rows, d), lambda l, n: (l, 0, n)),
        out_shape=jax.ShapeDtypeStruct((depth, rows, nd), F32),
        compiler_params=_cparams(("arbitrary", "arbitrary")),
        name="adaln_mod",
    )(cs, w_mod, b_mod.reshape(depth, nd // d, 1, d))


def _inproj_kernel(x_ref, mod_ref, g_ref, w_ref, wgt_ref, gbc_ref, gbt_ref, z_ref, zgt_ref):
    xm = _rms_modulate(x_ref[0], g_ref[...], mod_ref[0, 0:1, :], mod_ref[0, 1:2, :])
    xb = xm.astype(BF16)
    for j in range(N_TILES // 2):
        r = _dot(xb, w_ref[:, j * MXU_N:(j + 1) * MXU_N])
        if 2 * j + 1 == T_CG:
            r = r + gbc_ref[...]
        z_ref[0, 2 * j] = r[:, :LANES]
        z_ref[0, 2 * j + 1] = r[:, LANES:]
    zgt_ref[0] = _dot_nt(wgt_ref[...], xb) + gbt_ref[...]


def _inproj_call(x, mod, g, w, wgt, gbc, gbt, tm):
    b, s, d = x.shape
    per_sample = mod.shape[0] == b
    mod_map = (lambda i, j: (i, 0, 0)) if per_sample else (lambda i, j: (0, 0, 0))
    const = lambda i, j: (0, 0)
    return pl.pallas_call(
        _inproj_kernel,
        grid=(b, s // tm),
        in_specs=[pl.BlockSpec((1, tm, d), lambda i, j: (i, j, 0)),
                  pl.BlockSpec((1, N_MOD, d), mod_map),
                  pl.BlockSpec((1, d), const),
                  pl.BlockSpec((d, N_TILES * LANES), const),
                  pl.BlockSpec((N_GATES, d), const),
                  pl.BlockSpec((1, 2 * LANES), const),
                  pl.BlockSpec((N_GATES, tm), const)],
        out_specs=[pl.BlockSpec((1, N_TILES, tm, LANES), lambda i, j: (i, 0, j, 0)),
                   pl.BlockSpec((1, N_GATES, tm), lambda i, j: (i, 0, j))],
        out_shape=[jax.ShapeDtypeStruct((b, N_TILES, s, LANES), F32),
                   jax.ShapeDtypeStruct((b, N_GATES, s), F32)],
        compiler_params=_cparams(("arbitrary", "arbitrary")),
        name="in_proj",
    )(x, mod, g, w, wgt, gbc, gbt[:, :tm])


def _mix_a_kernel(u_ref, v_ref, lng_ref, lnb_ref, ws_ref, bsx_ref, avg_ref, o_ref, *, n_chunks):
    lane = lax.broadcasted_iota(jnp.int32, (A_CHUNK, LANES), 1)
    avg = avg_ref[...]
    for j in range(2):
        v = _gelu_tanh(v_ref[0, j])
        hi = v.astype(BF16)
        mu = _dot(hi, avg) + _dot((v - hi.astype(F32)).astype(BF16), avg)
        dv = v - mu
        sq = dv * dv
        hi = sq.astype(BF16)
        var = _dot(hi, avg) + _dot((sq - hi.astype(F32)).astype(BF16), avg)
        vn = (dv * lax.rsqrt(var + EPS) * lng_ref[j] + lnb_ref[j]).astype(BF16)
        for c in range(n_chunks):
            rows = slice(c * A_CHUNK, (c + 1) * A_CHUNK)
            vc = vn[rows]
            s = jnp.where(lane < HEAD_DIM, _dot(ws_ref[2 * j], vc), _dot(ws_ref[2 * j + 1], vc)) + bsx_ref[j]
            o_ref[0, j, rows, :] = _gelu_tanh(u_ref[0, j, rows, :]) * s


def _mix_a_call(z, lng, lnb, ws, bsx, avg, t):
    b, _, s, _ = z.shape
    c3 = lambda i, j: (0, 0, 0)
    return pl.pallas_call(
        functools.partial(_mix_a_kernel, n_chunks=t // A_CHUNK),
        grid=(b, s // t),
        in_specs=[pl.BlockSpec((1, 2, t, LANES), lambda i, j: (i, T_AU // 2, j, 0)),
                  pl.BlockSpec((1, 2, t, LANES), lambda i, j: (i, T_AV // 2, j, 0)),
                  pl.BlockSpec((2, 1, LANES), c3),
                  pl.BlockSpec((2, 1, LANES), c3),
                  pl.BlockSpec((A_GROUPS, A_CHUNK, A_CHUNK), c3),
                  pl.BlockSpec((2, A_CHUNK, LANES), c3),
                  pl.BlockSpec((LANES, LANES), lambda i, j: (0, 0))],
        out_specs=pl.BlockSpec((1, 2, t, LANES), lambda i, j: (i, 0, j, 0)),
        out_shape=jax.ShapeDtypeStruct((b, 2, s, LANES), F32),
        compiler_params=_cparams(("arbitrary", "arbitrary")),
        name="mix_a_gmlp",
    )(z, z, lng, lnb, ws, bsx, avg)


def _mix_b_kernel(q_ref, k_ref, v_ref, kc_ref, vc_ref, bias_ref, o_ref, *, n_rows):
    rb = pl.program_id(2)
    n_blocks = n_rows // Q_ROWS
    k_start = pl.multiple_of(jnp.clip(rb * Q_ROWS - WIN_ROWS // 2, 0, n_rows - K_ROWS) * GRID_W, Q_ROWS * GRID_W)
    cls = jnp.where(rb == 0, 0, jnp.where(rb == n_blocks - 1, 2, 1))
    nq = Q_ROWS * GRID_W
    nk = K_ROWS * GRID_W
    q = q_ref[0, 0] * (HEAD_DIM ** -0.5)
    kw = k_ref[0, 0, pl.ds(k_start, nk), :].astype(BF16)
    vw = v_ref[0, 0, pl.ds(k_start, nk), :].astype(BF16)
    kc = kc_ref[0, 0].astype(BF16)
    vc = vc_ref[0, 0].astype(BF16)
    lane = lax.broadcasted_iota(jnp.int32, (nq, LANES), 1)
    outs = []
    for e in range(2):
        qe = jnp.where((lane < HEAD_DIM) == (e == 0), q, 0.0).astype(BF16)
        s_w = _dot_nt(qe, kw) + bias_ref[0, e, cls]
        s_c = _dot_nt(qe, kc)
        m = jnp.maximum(jnp.max(s_w, axis=-1, keepdims=True), jnp.max(s_c, axis=-1, keepdims=True))
        p_w = jnp.exp(s_w - m)
        p_c = jnp.exp(s_c - m)
        den = jnp.sum(p_w, axis=-1, keepdims=True) + jnp.sum(p_c, axis=-1, keepdims=True)
        o = _dot(p_w.astype(BF16), vw) + _dot(p_c.astype(BF16), vc)
        outs.append(o / den)
    o_ref[0, 0] = jnp.where(lane < HEAD_DIM, outs[0], outs[1])


def _mix_b_call(z, zc, bias):
    b, _, s, _ = z.shape
    cl = zc.shape[2]
    n_rows = s // GRID_W
    nq = Q_ROWS * GRID_W
    return pl.pallas_call(
        functools.partial(_mix_b_kernel, n_rows=n_rows),
        grid=(N_PAIRS, b, s // nq),
        in_specs=[pl.BlockSpec((1, 1, nq, LANES), lambda p, i, r: (i, T_BQ + p, r, 0)),
                  pl.BlockSpec((1, 1, s, LANES), lambda p, i, r: (i, T_BK + p, 0, 0)),
                  pl.BlockSpec((1, 1, s, LANES), lambda p, i, r: (i, T_BV + p, 0, 0)),
                  pl.BlockSpec((1, 1, cl, LANES), lambda p, i, r: (i, T_BK + p, 0, 0)),
                  pl.BlockSpec((1, 1, cl, LANES), lambda p, i, r: (i, T_BV + p, 0, 0)),
                  pl.BlockSpec((1, 2, 3, nq, K_ROWS * GRID_W), lambda p, i, r: (p, 0, 0, 0, 0))],
        out_specs=pl.BlockSpec((1, 1, nq, LANES), lambda p, i, r: (i, p, r, 0)),
        out_shape=jax.ShapeDtypeStruct((b, N_PAIRS, s, LANES), F32),
        compiler_params=_cparams(("arbitrary", "arbitrary", "arbitrary")),
        name="mix_b_natten",
    )(z, z, z, zc, zc, bias)


def _mix_b_ctx_kernel(q_ref, k_ref, v_ref, o_ref):
    q = q_ref[0, 0] * (HEAD_DIM ** -0.5)
    k = k_ref[0, 0].astype(BF16)
    v = v_ref[0, 0].astype(BF16)
    lane = lax.broadcasted_iota(jnp.int32, q.shape, 1)
    outs = []
    for e in range(2):
        qe = jnp.where((lane < HEAD_DIM) == (e == 0), q, 0.0).astype(BF16)
        s = _dot_nt(qe, k)
        p = jnp.exp(s - jnp.max(s, axis=-1, keepdims=True))
        outs.append(_dot(p.astype(BF16), v) / jnp.sum(p, axis=-1, keepdims=True))
    o_ref[0, 0] = jnp.where(lane < HEAD_DIM, outs[0], outs[1])


def _mix_b_ctx_call(zc):
    b, _, cl, _ = zc.shape
    blk = lambda t: pl.BlockSpec((1, 1, cl, LANES), lambda i, p: (i, t + p, 0, 0))
    return pl.pallas_call(
        _mix_b_ctx_kernel,
        grid=(b, N_PAIRS),
        in_specs=[blk(T_BQ), blk(T_BK), blk(T_BV)],
        out_specs=blk(0),
        out_shape=jax.ShapeDtypeStruct((b, N_PAIRS, cl, LANES), F32),
        compiler_params=_cparams(("arbitrary", "arbitrary")),
        name="mix_b_ctx",
    )(zc, zc, zc)


def _natten_bias(rpb, n_rows):
    r0s = np.array([0, Q_ROWS, n_rows - Q_ROWS])
    ks = np.clip(r0s - WIN_ROWS // 2, 0, n_rows - K_ROWS)
    r = (r0s[:, None] + np.arange(Q_ROWS)[None, :])[:, :, None, None, None]
    c = np.arange(GRID_W)[None, None, :, None, None]
    kr = (ks[:, None] + np.arange(K_ROWS)[None, :])[:, None, None, :, None]
    kc = np.arange(GRID_W)[None, None, None, None, :]
    rs = np.clip(r - WIN_ROWS // 2, 0, n_rows - WIN_ROWS)
    cs = np.clip(c - WIN_COLS // 2, 0, GRID_W - WIN_COLS)
    valid = (kr >= rs) & (kr < rs + WIN_ROWS) & (kc >= cs) & (kc < cs + WIN_COLS)
    di = np.broadcast_to(np.clip(kr - r + WIN_ROWS - 1, 0, 2 * WIN_ROWS - 2), valid.shape)
    dj = np.broadcast_to(np.clip(kc - c + WIN_COLS - 1, 0, 2 * WIN_COLS - 2), valid.shape)
    bias = jnp.where(valid[None], rpb[:, di, dj], NEG)
    bias = bias.reshape(N_PAIRS, 2, 3, Q_ROWS * GRID_W, K_ROWS * GRID_W)
    return bias.astype(F32)


def _mlstm_chunk(qb, kb, v, xg, rg, ep_ref, tri, triu, st, m_pl, direction, lane, bd_mask, tri_mask):
    n_l = qb.shape[0]
    ki, kf = 2 * direction, 2 * direction + 1
    i_pl = _dot3_r(xg, ep_ref[0, ki])
    f_pl = _dot3_r(xg, ep_ref[0, kf])
    fc_pl = _dot3_l(tri, f_pl)
    f_row = _dot3_r(rg, triu)
    f_tot = fc_pl[n_l - 1:n_l, :] if direction == 0 else fc_pl[0:1, :]

    log_inter = fc_pl + m_pl
    p_heads, mt_heads = [], []
    for e in range(2):
        fcol = fc_pl[:, e * HEAD_DIM:e * HEAD_DIM + 1]
        b_row = rg[2 * ki + e:2 * ki + e + 1, :] - f_row[2 * kf + e:2 * kf + e + 1, :]
        d = jnp.where(tri_mask, fcol + b_row, NEG)
        m_e = m_pl[:, e * HEAD_DIM:e * HEAD_DIM + 1]
        mt = jnp.maximum(fcol + m_e, jnp.max(d, axis=-1, keepdims=True))
        head_lanes = jnp.where((lane[0:1, :] < HEAD_DIM) == (e == 0), 1.0, 0.0).astype(BF16)
        s = _dot_nt(qb * head_lanes, kb)
        p_heads.append((s * jnp.exp(d - mt)).astype(BF16))
        mt_heads.append(mt)
    mt_pl = jnp.where(lane < HEAD_DIM, mt_heads[0], mt_heads[1])
    w_inter = jnp.exp(log_inter - mt_pl)

    ones = jnp.ones_like(v)
    zero = jnp.zeros_like(v)
    head0 = lane < HEAD_DIM
    vext = jnp.concatenate(
        [jnp.concatenate([jnp.where(head0, v, zero), jnp.where(head0, ones, zero)], axis=1),
         jnp.concatenate([jnp.where(head0, zero, v), jnp.where(head0, zero, ones)], axis=1)], axis=0).astype(BF16)
    intra = _dot(jnp.concatenate(p_heads, axis=1), vext)
    inter = _dot(qb, st.astype(BF16))
    num = w_inter * inter[:, :LANES] + intra[:, :LANES]
    den = w_inter * inter[:, LANES:] + intra[:, LANES:]
    h = num / jnp.maximum(jnp.abs(den), jnp.exp(-mt_pl))

    log_w = f_tot - fc_pl + i_pl
    m_new = jnp.maximum(f_tot + m_pl, jnp.max(log_w, axis=0, keepdims=True))
    a_prev = jnp.exp(f_tot + m_pl - m_new)
    a_s = jnp.exp(log_w - m_new)
    upd = _dot_tn(kb, jnp.concatenate([a_s * v, a_s], axis=1).astype(BF16))
    st_new = jnp.concatenate([a_prev, a_prev], axis=1) * st + jnp.where(bd_mask, upd, 0.0)
    return h, st_new, m_new


def _mix_c_kernel(zq_ref, zk_ref, zv_ref, zo_ref, xg_ref, rg_ref,
                  cq_ref, ck_ref, cv_ref, co_ref, cxg_ref, crg_ref,
                  wq_ref, wk_ref, bq_ref, bk_ref, cos_ref, sin_ref, ep_ref, tri_ref,
                  o_ref, oc_ref, qs_ref, ks_ref, gs_ref, cqs_ref, cks_ref, cgs_ref):
    s_len = zq_ref.shape[2]
    n_l = C_LEN
    n_chunks = s_len // n_l
    lane = lax.broadcasted_iota(jnp.int32, (n_l, LANES), 1)
    row = lax.broadcasted_iota(jnp.int32, (n_l, LANES), 0)
    gate_kind = (lane % 8) // 2
    is_forget = (lane < N_GATES) & ((gate_kind == 1) | (gate_kind == 3))
    k_scale = HEAD_DIM ** -0.5

    def conv_silu(x_ref, t0, total, w_ref, b_ref):
        x = x_ref[0, 0, pl.ds(t0, n_l), :]
        if total == n_l:
            prev_row = next_row = jnp.zeros((1, LANES), F32)
        else:
            prev_row = x_ref[0, 0, pl.ds(jnp.maximum(t0 - 1, 0), 1), :] * (t0 > 0).astype(F32)
            next_row = x_ref[0, 0, pl.ds(jnp.minimum(t0 + n_l, total - 1), 1), :] * (t0 + n_l < total).astype(F32)
        x_prev = jnp.where(row == 0, prev_row, pltpu.roll(x, 1, axis=0))
        x_next = jnp.where(row == n_l - 1, next_row, pltpu.roll(x, n_l - 1, axis=0))
        y = x_prev * w_ref[0, 0:1, :] + x * w_ref[0, 1:2, :] + x_next * w_ref[0, 2:3, :] + b_ref[0]
        return _silu(y)

    def rope(x, t0):
        cos = cos_ref[pl.ds(t0, n_l), :]
        sin = sin_ref[pl.ds(t0, n_l), :]
        first = (lane % (HEAD_DIM // 2)) < (HEAD_DIM // 4)
        partner = jnp.where(first, pltpu.roll(x, LANES - HEAD_DIM // 4, axis=1), pltpu.roll(x, HEAD_DIM // 4, axis=1))
        return x * cos + partner * sin

    def gate_cols(x):
        return jnp.where(is_forget, _log_sigmoid(x), x)

    def gate_rows(r):
        kind = lax.broadcasted_iota(jnp.int32, r.shape, 0) // 2
        return jnp.where((kind == 1) | (kind == 3), _log_sigmoid(r), r)

    cqs_ref[...] = conv_silu(cq_ref, 0, n_l, wq_ref, bq_ref).astype(BF16)
    cks_ref[...] = (conv_silu(ck_ref, 0, n_l, wk_ref, bk_ref) * k_scale).astype(BF16)
    cgs_ref[...] = gate_cols(cxg_ref[0, 0])

    def prep(c, carry):
        t0 = pl.multiple_of(c * n_l, n_l)
        qs_ref[pl.ds(t0, n_l), :] = rope(conv_silu(zq_ref, t0, s_len, wq_ref, bq_ref), t0).astype(BF16)
        ks_ref[pl.ds(t0, n_l), :] = (rope(conv_silu(zk_ref, t0, s_len, wk_ref, bk_ref), t0) * k_scale).astype(BF16)
        gs_ref[pl.ds(t0, n_l), :] = gate_cols(xg_ref[0, 0, pl.ds(t0, n_l), :])
        return carry

    lax.fori_loop(0, n_chunks, prep, 0)

    r_i = lax.broadcasted_iota(jnp.int32, (n_l, n_l), 0)
    c_i = lax.broadcasted_iota(jnp.int32, (n_l, n_l), 1)
    bd_r = lax.broadcasted_iota(jnp.int32, (LANES, 2 * LANES), 0)
    bd_c = lax.broadcasted_iota(jnp.int32, (LANES, 2 * LANES), 1)
    bd_mask = (bd_r < HEAD_DIM) == ((bd_c % LANES) < HEAD_DIM)
    crg = gate_rows(crg_ref[0, 0])

    for direction in range(2):
        tri = tri_ref[direction]
        triu = tri_ref[1 - direction]
        tri_mask = (c_i <= r_i) if direction == 0 else (c_i >= r_i)
        args = (ep_ref, tri, triu)
        st0 = jnp.zeros((LANES, 2 * LANES), F32)
        m0 = jnp.zeros((1, LANES), F32)
        hc, st, m_pl = _mlstm_chunk(cqs_ref[...], cks_ref[...], cv_ref[0, 0], cgs_ref[...], crg, *args,
                                    st0, m0, direction, lane, bd_mask, tri_mask)
        if direction == 0:
            oc_ref[0, 0] = hc
        else:
            oc_ref[0, 0] = (oc_ref[0, 0] + hc) * _sigmoid(co_ref[0, 0])

        def step(i, carry, direction=direction, args=args, tri_mask=tri_mask):
            st, m_pl = carry
            c = i if direction == 0 else n_chunks - 1 - i
            t0 = pl.multiple_of(c * n_l, n_l)
            rows = pl.ds(t0, n_l)
            rg = gate_rows(rg_ref[0, 0, :, rows])
            h, st, m_pl = _mlstm_chunk(qs_ref[rows, :], ks_ref[rows, :], zv_ref[0, 0, rows, :], gs_ref[rows, :], rg,
                                       *args, st, m_pl, direction, lane, bd_mask, tri_mask)
            if direction == 0:
                o_ref[0, 0, rows, :] = h
            else:
                o_ref[0, 0, rows, :] = (o_ref[0, 0, rows, :] + h) * _sigmoid(zo_ref[0, 0, rows, :])
            return st, m_pl

        lax.fori_loop(0, n_chunks, step, (st, m_pl))


def _mix_c_call(z, zg, zc, zcg, wconv, bconv, cos_t, sin_t, ep, tri):
    b, _, s, _ = z.shape
    cl = zc.shape[2]
    assert cl == C_LEN and s % C_LEN == 0
    zt = lambda t: pl.BlockSpec((1, 1, s, LANES), lambda i, p: (i, t + p, 0, 0))
    ct = lambda t: pl.BlockSpec((1, 1, cl, LANES), lambda i, p: (i, t + p, 0, 0))
    return pl.pallas_call(
        _mix_c_kernel,
        grid=(b, N_PAIRS),
        in_specs=[zt(T_CQ), zt(T_CK), zt(T_CV), zt(T_CO),
                  pl.BlockSpec((1, 1, s, LANES), lambda i, p: (i, T_CG, 0, 0)),
                  pl.BlockSpec((1, 1, 8, s), lambda i, p: (i, p, 0, 0)),
                  ct(T_CQ), ct(T_CK), ct(T_CV), ct(T_CO),
                  pl.BlockSpec((1, 1, cl, LANES), lambda i, p: (i, T_CG, 0, 0)),
                  pl.BlockSpec((1, 1, 8, cl), lambda i, p: (i, p, 0, 0)),
                  pl.BlockSpec((1, 3, LANES), lambda i, p: (p, 0, 0)),
                  pl.BlockSpec((1, 3, LANES), lambda i, p: (N_PAIRS + p, 0, 0)),
                  pl.BlockSpec((1, 1, LANES), lambda i, p: (p, 0, 0)),
                  pl.BlockSpec((1, 1, LANES), lambda i, p: (N_PAIRS + p, 0, 0)),
                  pl.BlockSpec((s, LANES), lambda i, p: (0, 0)),
                  pl.BlockSpec((s, LANES), lambda i, p: (0, 0)),
                  pl.BlockSpec((1, 4, LANES, LANES), lambda i, p: (p, 0, 0, 0)),
                  pl.BlockSpec((2, C_LEN, C_LEN), lambda i, p: (0, 0, 0))],
        out_specs=[pl.BlockSpec((1, 1, s, LANES), lambda i, p: (i, p, 0, 0)),
                   pl.BlockSpec((1, 1, cl, LANES), lambda i, p: (i, p, 0, 0))],
        out_shape=[jax.ShapeDtypeStruct((b, N_PAIRS, s, LANES), F32),
                   jax.ShapeDtypeStruct((b, N_PAIRS, cl, LANES), F32)],
        scratch_shapes=[pltpu.VMEM((s, LANES), BF16), pltpu.VMEM((s, LANES), BF16), pltpu.VMEM((s, LANES), F32),
                        pltpu.VMEM((cl, LANES), BF16), pltpu.VMEM((cl, LANES), BF16), pltpu.VMEM((cl, LANES), F32)],
        compiler_params=_cparams(("arbitrary", "arbitrary")),
        name="mix_c_mlstm",
    )(z, z, z, z, z, zg.reshape(b, N_PAIRS, 8, s), zc, zc, zc, zc, zc, zcg.reshape(b, N_PAIRS, 8, cl),
      wconv, wconv, bconv, bconv, cos_t, sin_t, ep, tri)


def _rope_tables(s):
    t = jnp.arange(s)
    quarter = HEAD_DIM // 4
    inv_freq = ROPE_BASE ** (-jnp.arange(quarter, dtype=F32) / quarter)
    lane = np.arange(LANES) % HEAD_DIM
    use_col = jnp.asarray(lane >= HEAD_DIM // 2)
    second = jnp.asarray((lane % (HEAD_DIM // 2)) >= quarter)
    freq = inv_freq[np.asarray(lane % quarter)]
    pos = jnp.where(use_col[None, :], (t % GRID_W)[:, None], (t // GRID_W)[:, None]).astype(F32)
    ang = pos * freq[None, :]
    return jnp.cos(ang), jnp.where(second[None, :], jnp.sin(ang), -jnp.sin(ang))


def _gate_tables():
    ep = np.zeros((N_PAIRS, 4, LANES, LANES), np.float32)
    for p in range(N_PAIRS):
        for k in range(4):
            for ln in range(LANES):
                ep[p, k, 8 * p + 2 * k + ln // HEAD_DIM, ln] = 1.0
    t = np.arange(C_LEN)
    tri = np.stack([(t[None, :] <= t[:, None]), (t[None, :] >= t[:, None])]).astype(np.float32)
    return jnp.asarray(ep, BF16), jnp.asarray(tri, BF16)


def _post_kernel(x_ref, ya_ref, yb_ref, yc_ref, mod_ref, wout_ref, g_ref, wg_ref, wu_ref, wd_ref, fg_ref, o_ref,
                 *, final, ff_chunk):
    y = jnp.concatenate([ya_ref[0, 0], ya_ref[0, 1], yb_ref[0, 0], yb_ref[0, 1], yb_ref[0, 2],
                         yc_ref[0, 0], yc_ref[0, 1], yc_ref[0, 2]], axis=1).astype(BF16)
    x1 = x_ref[0] + mod_ref[0, 2:3, :] * _dot(y, wout_ref[...])
    hm = _rms_modulate(x1, g_ref[...], mod_ref[0, 3:4, :], mod_ref[0, 4:5, :]).astype(BF16)
    acc = jnp.zeros_like(x1)
    for c in range(wg_ref.shape[1] // ff_chunk):
        cols = slice(c * ff_chunk, (c + 1) * ff_chunk)
        a = _silu(_dot(hm, wg_ref[:, cols])) * _dot(hm, wu_ref[:, cols])
        acc = acc + _dot(a.astype(BF16), wd_ref[cols, :])
    x2 = x1 + mod_ref[0, 5:6, :] * acc
    if final:
        ms = jnp.mean(x2 * x2, axis=-1, keepdims=True)
        x2 = x2 * lax.rsqrt(ms + EPS) * fg_ref[...]
    o_ref[0] = x2


def _post_call(x, ya, yb, yc, mod, wout, g, wg, wu, wd, fg, tm, final):
    b, s, d = x.shape
    dff = wg.shape[1]
    per_sample = mod.shape[0] == b
    mod_map = (lambda i, j: (i, 0, 0)) if per_sample else (lambda i, j: (0, 0, 0))
    const = lambda i, j: (0, 0)
    once = dict(pipeline_mode=pl.Buffered(1))
    yt = lambda n: pl.BlockSpec((1, n, tm, LANES), lambda i, j: (i, 0, j, 0))
    return pl.pallas_call(
        functools.partial(_post_kernel, final=final, ff_chunk=MXU_N),
        grid=(b, s // tm),
        in_specs=[pl.BlockSpec((1, tm, d), lambda i, j: (i, j, 0)),
                  yt(2), yt(N_PAIRS), yt(N_PAIRS),
                  pl.BlockSpec((1, N_MOD, d), mod_map),
                  pl.BlockSpec((d, d), const, **once),
                  pl.BlockSpec((1, d), const),
                  pl.BlockSpec((d, dff), const, **once),
                  pl.BlockSpec((d, dff), const, **once),
                  pl.BlockSpec((dff, d), const, **once),
                  pl.BlockSpec((1, d), const)],
        out_specs=pl.BlockSpec((1, tm, d), lambda i, j: (i, j, 0)),
        out_shape=jax.ShapeDtypeStruct((b, s, d), F32),
        compiler_params=_cparams(("arbitrary", "arbitrary")),
        name="out_proj_swiglu",
    )(x, ya, yb, yc, mod, wout, g, wg, wu, wd, fg)


def _gate_perm():
    perm = np.zeros(N_GATES, np.int32)
    for p in range(N_PAIRS):
        for k in range(4):
            for e in range(2):
                perm[8 * p + 2 * k + e] = C_HEADS * k + 2 * p + e
    return perm


def kernel(x, c, ctx, c_ctx, w_mod, b_mod, norm1_g, w_in, a_ln_g, a_ln_b, a_ws, a_bs, b_rpb,
           c_conv_w, c_conv_b, c_gate_b, w_out, norm2_g, w_gate, w_up, w_down, final_g):
    b, s, d = x.shape
    cl = ctx.shape[1]
    depth = w_mod.shape[0]
    n_main = T_CG * LANES
    tm = 512
    perm = _gate_perm()

    pad = (-(b + 1)) % 8
    cs = jnp.concatenate([c, c_ctx[None, :], jnp.zeros((pad, d), F32)], axis=0)
    mod_all = _mod_call(cs, w_mod.astype(BF16), b_mod)

    cos_t, sin_t = _rope_tables(s)
    ep, tri = _gate_tables()
    avg = jnp.asarray(np.kron(np.eye(2), np.full((HEAD_DIM, HEAD_DIM), 1.0 / HEAD_DIM)), BF16)

    xc = ctx
    for l in range(depth):
        last = l == depth - 1
        mod = mod_all[l, :b].reshape(b, N_MOD, d)
        mod_c = mod_all[l, b:b + 1].reshape(1, N_MOD, d)

        wg_perm = w_in[l][:, n_main:][:, perm]
        w_main = jnp.concatenate([w_in[l][:, :n_main], wg_perm, jnp.zeros((d, LANES - N_GATES), F32)], axis=1)
        w_main = w_main.astype(BF16)
        wgt = wg_perm.T.astype(BF16)
        gb = c_gate_b[l].reshape(-1)[perm]
        gbc = jnp.concatenate([jnp.zeros((LANES,), F32), gb, jnp.zeros((LANES - N_GATES,), F32)])[None, :]
        gbt = jnp.broadcast_to(gb[:, None], (N_GATES, tm))
        g1 = norm1_g[l][None, :]

        z, zg = _inproj_call(x, mod, g1, w_main, wgt, gbc, gbt, tm)
        zc, zcg = _inproj_call(xc, mod_c, g1, w_main, wgt, gbc, gbt, cl)

        lng = a_ln_g[l].reshape(2, 1, LANES)
        lnb = a_ln_b[l].reshape(2, 1, LANES)
        ws = a_ws[l].astype(BF16)
        bsx = jnp.repeat(a_bs[l].reshape(2, 2, A_CHUNK).transpose(0, 2, 1), HEAD_DIM, axis=2)
        ya = _mix_a_call(z, lng, lnb, ws, bsx, avg, tm)

        yb = _mix_b_call(z, zc, _natten_bias(b_rpb[l], s // GRID_W))

        wconv = c_conv_w[l].reshape(3, 2 * N_PAIRS, LANES).transpose(1, 0, 2)
        bconv = c_conv_b[l].reshape(2 * N_PAIRS, 1, LANES)
        yc, ycc = _mix_c_call(z, zg, zc, zcg, wconv, bconv, cos_t, sin_t, ep, tri)

        weights = (w_out[l].astype(BF16), norm2_g[l][None, :], w_gate[l].astype(BF16), w_up[l].astype(BF16),
                   w_down[l].astype(BF16), final_g[None, :])
        x = _post_call(x, ya, yb, yc, mod, *weights, tm, last)
        if not last:
            yac = _mix_a_call(zc, lng, lnb, ws, bsx, avg, cl)
            ybc = _mix_b_ctx_call(zc)
            xc = _post_call(xc, yac, ybc, ycc, mod_c, *weights, cl, False)
    return x
```

```python
import functools

import numpy as np
import jax
import jax.numpy as jnp
from jax import lax
from jax.experimental import pallas as pl
from jax.experimental.pallas import tpu as pltpu

F32 = jnp.float32
BF16 = jnp.bfloat16

LANES = 128
MXU_N = 256
HEAD_DIM = 64
GRID_W = 64
A_GROUPS = 4
A_CHUNK = 128
B_HEADS = 6
C_HEADS = 6
WIN_ROWS = 8
WIN_COLS = 16
ROPE_BASE = 10000.0
EPS = 1e-6
N_MOD = 6
NEG = -1e30
LOG2E = float(np.log2(np.e))
LN2 = float(np.log(2.0))

T_AU, T_AV, T_BQ, T_BK, T_BV, T_CQ, T_CK, T_CV, T_CO, T_CG = 0, 2, 4, 7, 10, 13, 16, 19, 22, 25
N_TILES = 26
N_PAIRS = 3
N_GATES = 4 * C_HEADS
Q_ROWS = 4
K_ROWS = Q_ROWS + WIN_ROWS
C_LEN = 256

VMEM_LIMIT = 56 * 1024 * 1024


def _cparams(sem):
    return pltpu.CompilerParams(dimension_semantics=sem, vmem_limit_bytes=VMEM_LIMIT)


def _dot(a, b):
    return jnp.dot(a, b, preferred_element_type=F32)


def _dot_nt(a, b):
    return lax.dot_general(a, b, (((1,), (1,)), ((), ())), preferred_element_type=F32)


def _split2(x):
    h = x.astype(BF16)
    return h, (x - h.astype(F32)).astype(BF16)


def _dot2_r(x, sel2):
    h, l = _split2(x)
    return _dot(jnp.concatenate([h, l], axis=1), sel2)


def _dot2_l(sel, x):
    n = x.shape[1]
    h, l = _split2(x)
    r = _dot(sel, jnp.concatenate([h, l], axis=1))
    return r[:, :n] + r[:, n:]


def _silu(x):
    h = 0.5 * x
    return h + h * jnp.tanh(h)


def _sigmoid(x):
    return 0.5 + 0.5 * jnp.tanh(0.5 * x)


def _gelu_tanh(x):
    return 0.5 * x * (1.0 + jnp.tanh(np.sqrt(2.0 / np.pi).astype(np.float32) * (x + 0.044715 * (x * x * x))))


def _log_sigmoid(x):
    return jnp.minimum(x, 0.0) - jnp.log1p(jnp.exp(-jnp.abs(x)))


def _rms_modulate(x, g, shift, scale):
    ms = jnp.mean(x * x, axis=-1, keepdims=True)
    return (x * lax.rsqrt(ms + EPS) * g) * (1.0 + scale) + shift


def _mod_kernel(c_ref, w_ref, b_ref, o_ref):
    o_ref[0] = _dot(_silu(c_ref[...]).astype(BF16), w_ref[0]) + b_ref[0, 0]


def _mod_call(cs, w_mod, b_mod):
    depth, d, nd = w_mod.shape
    rows = cs.shape[0]
    return pl.pallas_call(
        _mod_kernel,
        grid=(depth, nd // d),
        in_specs=[pl.BlockSpec((rows, d), lambda l, n: (0, 0)),
                  pl.BlockSpec((1, d, d), lambda l, n: (l, 0, n)),
                  pl.BlockSpec((1, 1, 1, d), lambda l, n: (l, n, 0, 0))],
        out_specs=pl.BlockSpec((1, rows, d), lambda l, n: (l, 0, n)),
        out_shape=jax.ShapeDtypeStruct((depth, rows, nd), F32),
        compiler_params=_cparams(("arbitrary", "arbitrary")),
        name="adaln_mod",
    )(cs, w_mod, b_mod.reshape(depth, nd // d, 1, d))


def _inproj_kernel(x_ref, mod_ref, g_ref, w_ref, wgt_ref, gbc_ref, gbt_ref, z_ref, zgt_ref):
    xm = _rms_modulate(x_ref[0], g_ref[...], mod_ref[0, 0:1, :], mod_ref[0, 1:2, :])
    xb = xm.astype(BF16)
    for j in range(N_TILES // 2):
        r = _dot(xb, w_ref[:, j * MXU_N:(j + 1) * MXU_N])
        z_ref[0, 2 * j] = r[:, :LANES]
        if 2 * j + 1 == T_CG:
            g = r[:, LANES:] + gbc_ref[...]
            lane = lax.broadcasted_iota(jnp.int32, g.shape, 1)
            z_ref[0, T_CG] = jnp.where((lane < N_GATES) & ((lane % 8) // 2 % 2 == 1), _log_sigmoid(g), g)
        else:
            z_ref[0, 2 * j + 1] = r[:, LANES:]
    g = _dot_nt(wgt_ref[...], xb) + gbt_ref[...]
    sub = lax.broadcasted_iota(jnp.int32, g.shape, 0)
    zgt_ref[0] = jnp.where(sub % 8 >= 4, _log_sigmoid(g), g)


def _inproj_call(x, mod, g, w, wgt, gbc, gbt, tm):
    b, s, d = x.shape
    per_sample = mod.shape[0] == b
    mod_map = (lambda i, j: (i, 0, 0)) if per_sample else (lambda i, j: (0, 0, 0))
    const = lambda i, j: (0, 0)
    return pl.pallas_call(
        _inproj_kernel,
        grid=(b, s // tm),
        in_specs=[pl.BlockSpec((1, tm, d), lambda i, j: (i, j, 0)),
                  pl.BlockSpec((1, N_MOD, d), mod_map),
                  pl.BlockSpec((1, d), const),
                  pl.BlockSpec((d, N_TILES * LANES), const),
                  pl.BlockSpec((N_GATES, d), const),
                  pl.BlockSpec((1, LANES), const),
                  pl.BlockSpec((N_GATES, tm), const)],
        out_specs=[pl.BlockSpec((1, N_TILES, tm, LANES), lambda i, j: (i, 0, j, 0)),
                   pl.BlockSpec((1, N_GATES, tm), lambda i, j: (i, 0, j))],
        out_shape=[jax.ShapeDtypeStruct((b, N_TILES, s, LANES), F32),
                   jax.ShapeDtypeStruct((b, N_GATES, s), F32)],
        compiler_params=_cparams(("arbitrary", "arbitrary")),
        name="in_proj",
    )(x, mod, g, w, wgt, gbc, gbt[:, :tm])


def _mix_a_kernel(u_ref, v_ref, lng_ref, lnb_ref, ws_ref, bsx_ref, avg_ref, o_ref, *, n_chunks):
    lane = lax.broadcasted_iota(jnp.int32, (A_CHUNK, LANES), 1)
    avg = avg_ref[...]
    for j in range(2):
        v = _gelu_tanh(v_ref[0, j])
        mu = _dot2_r(v, avg)
        dv = v - mu
        var = _dot2_r(dv * dv, avg)
        vn = (dv * lax.rsqrt(var + EPS) * lng_ref[j] + lnb_ref[j]).astype(BF16)
        for c in range(n_chunks):
            rows = slice(c * A_CHUNK, (c + 1) * A_CHUNK)
            vc = vn[rows]
            s = jnp.where(lane < HEAD_DIM, _dot(ws_ref[2 * j], vc), _dot(ws_ref[2 * j + 1], vc)) + bsx_ref[j]
            o_ref[0, j, rows, :] = _gelu_tanh(u_ref[0, j, rows, :]) * s


def _mix_a_call(z, lng, lnb, ws, bsx, avg, t):
    b, _, s, _ = z.shape
    c3 = lambda i, j: (0, 0, 0)
    return pl.pallas_call(
        functools.partial(_mix_a_kernel, n_chunks=t // A_CHUNK),
        grid=(b, s // t),
        in_specs=[pl.BlockSpec((1, 2, t, LANES), lambda i, j: (i, T_AU // 2, j, 0)),
                  pl.BlockSpec((1, 2, t, LANES), lambda i, j: (i, T_AV // 2, j, 0)),
                  pl.BlockSpec((2, 1, LANES), c3),
                  pl.BlockSpec((2, 1, LANES), c3),
                  pl.BlockSpec((A_GROUPS, A_CHUNK, A_CHUNK), c3),
                  pl.BlockSpec((2, A_CHUNK, LANES), c3),
                  pl.BlockSpec((2 * LANES, LANES), lambda i, j: (0, 0))],
        out_specs=pl.BlockSpec((1, 2, t, LANES), lambda i, j: (i, 0, j, 0)),
        out_shape=jax.ShapeDtypeStruct((b, 2, s, LANES), F32),
        compiler_params=_cparams(("arbitrary", "arbitrary")),
        name="mix_a_gmlp",
    )(z, z, lng, lnb, ws, bsx, avg)


def _mix_b_kernel(q_ref, k_ref, v_ref, kc_ref, vc_ref, bias_ref, o_ref, kb_ref, vb_ref, *, n_rows):
    n_blocks = n_rows // Q_ROWS
    nq = Q_ROWS * GRID_W
    nk = K_ROWS * GRID_W
    kb_ref[...] = k_ref[0, 0].astype(BF16)
    vb_ref[...] = v_ref[0, 0].astype(BF16)
    kc = kc_ref[0, 0].astype(BF16)
    vc = vc_ref[0, 0].astype(BF16)
    lane = lax.broadcasted_iota(jnp.int32, (nq, LANES), 1)

    def block(rb, carry):
        q_start = pl.multiple_of(rb * nq, nq)
        k_start = pl.multiple_of(jnp.clip(rb * Q_ROWS - WIN_ROWS // 2, 0, n_rows - K_ROWS) * GRID_W, nq)
        cls = jnp.where(rb == 0, 0, jnp.where(rb == n_blocks - 1, 2, 1))
        q = q_ref[0, 0, pl.ds(q_start, nq), :] * (HEAD_DIM ** -0.5 * LOG2E)
        kw = kb_ref[pl.ds(k_start, nk), :]
        vw = vb_ref[pl.ds(k_start, nk), :]
        outs = []
        for e in range(2):
            qe = jnp.where((lane < HEAD_DIM) == (e == 0), q, 0.0).astype(BF16)
            s_w = _dot_nt(qe, kw) + bias_ref[0, e, cls]
            s_c = _dot_nt(qe, kc)
            m = jnp.maximum(jnp.max(s_w, axis=-1, keepdims=True), jnp.max(s_c, axis=-1, keepdims=True))
            p_w = jnp.exp2(s_w - m)
            p_c = jnp.exp2(s_c - m)
            den = jnp.sum(p_w, axis=-1, keepdims=True) + jnp.sum(p_c, axis=-1, keepdims=True)
            o = _dot(p_w.astype(BF16), vw) + _dot(p_c.astype(BF16), vc)
            outs.append(o / den)
        o_ref[0, 0, pl.ds(q_start, nq), :] = jnp.where(lane < HEAD_DIM, outs[0], outs[1])
        return carry

    lax.fori_loop(0, n_blocks, block, 0, unroll=2)


def _mix_b_call(z, zc, bias):
    b, _, s, _ = z.shape
    cl = zc.shape[2]
    n_rows = s // GRID_W
    assert n_rows % (2 * Q_ROWS) == 0 and n_rows >= K_ROWS
    zt = lambda t: pl.BlockSpec((1, 1, s, LANES), lambda p, i: (i, t + p, 0, 0))
    ct = lambda t: pl.BlockSpec((1, 1, cl, LANES), lambda p, i: (i, t + p, 0, 0))
    return pl.pallas_call(
        functools.partial(_mix_b_kernel, n_rows=n_rows),
        grid=(N_PAIRS, b),
        in_specs=[zt(T_BQ), zt(T_BK), zt(T_BV), ct(T_BK), ct(T_BV),
                  pl.BlockSpec((1, 2, 3, Q_ROWS * GRID_W, K_ROWS * GRID_W), lambda p, i: (p, 0, 0, 0, 0))],
        out_specs=pl.BlockSpec((1, 1, s, LANES), lambda p, i: (i, p, 0, 0)),
        out_shape=jax.ShapeDtypeStruct((b, N_PAIRS, s, LANES), F32),
        scratch_shapes=[pltpu.VMEM((s, LANES), BF16), pltpu.VMEM((s, LANES), BF16)],
        compiler_params=_cparams(("arbitrary", "arbitrary")),
        name="mix_b_natten",
    )(z, z, z, zc, zc, bias)


def _mix_b_ctx_kernel(q_ref, k_ref, v_ref, o_ref):
    q = q_ref[0, 0] * (HEAD_DIM ** -0.5)
    k = k_ref[0, 0].astype(BF16)
    v = v_ref[0, 0].astype(BF16)
    lane = lax.broadcasted_iota(jnp.int32, q.shape, 1)
    outs = []
    for e in range(2):
        qe = jnp.where((lane < HEAD_DIM) == (e == 0), q, 0.0).astype(BF16)
        s = _dot_nt(qe, k)
        p = jnp.exp(s - jnp.max(s, axis=-1, keepdims=True))
        outs.append(_dot(p.astype(BF16), v) / jnp.sum(p, axis=-1, keepdims=True))
    o_ref[0, 0] = jnp.where(lane < HEAD_DIM, outs[0], outs[1])


def _mix_b_ctx_call(zc):
    b, _, cl, _ = zc.shape
    blk = lambda t: pl.BlockSpec((1, 1, cl, LANES), lambda i, p: (i, t + p, 0, 0))
    return pl.pallas_call(
        _mix_b_ctx_kernel,
        grid=(b, N_PAIRS),
        in_specs=[blk(T_BQ), blk(T_BK), blk(T_BV)],
        out_specs=blk(0),
        out_shape=jax.ShapeDtypeStruct((b, N_PAIRS, cl, LANES), F32),
        compiler_params=_cparams(("arbitrary", "arbitrary")),
        name="mix_b_ctx",
    )(zc, zc, zc)


def _natten_bias(rpb, n_rows):
    h = rpb.shape[0]
    padded = jnp.pad(rpb, ((0, 0), (0, 0), (GRID_W, GRID_W)))
    toe = jnp.stack([padded[:, :, GRID_W + WIN_COLS - 1 - c:2 * GRID_W + WIN_COLS - 1 - c] for c in range(GRID_W)], axis=2)
    c = np.arange(GRID_W)[:, None]
    kc = np.arange(GRID_W)[None, :]
    cs = np.clip(c - WIN_COLS // 2, 0, GRID_W - WIN_COLS)
    col_ok = (kc >= cs) & (kc < cs + WIN_COLS)
    toe = jnp.where(col_ok[None, None], toe, NEG)
    neg_block = jnp.full((h, GRID_W, GRID_W), NEG, F32)
    classes = []
    for r0 in (0, Q_ROWS, n_rows - Q_ROWS):
        k0 = int(np.clip(r0 - WIN_ROWS // 2, 0, n_rows - K_ROWS))
        q_rows = []
        for i in range(Q_ROWS):
            r = r0 + i
            rs = int(np.clip(r - WIN_ROWS // 2, 0, n_rows - WIN_ROWS))
            blocks = []
            for j in range(K_ROWS):
                kr = k0 + j
                blocks.append(toe[:, kr - r + WIN_ROWS - 1] if rs <= kr < rs + WIN_ROWS else neg_block)
            q_rows.append(jnp.concatenate(blocks, axis=-1))
        classes.append(jnp.concatenate(q_rows, axis=1))
    bias = jnp.stack(classes, axis=1)
    return bias.reshape(N_PAIRS, 2, 3, Q_ROWS * GRID_W, K_ROWS * GRID_W)


def _mlstm_chunk(qb, kt, v, fc, w, b_rows, dmask, st, m_pl, direction, head0, head_lanes, ones_blk, bd_mask):
    n_l = qb.shape[0]
    f_tot = fc[n_l - 1:n_l, :] if direction == 0 else fc[0:1, :]
    p_heads, cm_heads = [], []
    s_heads = _dot(jnp.concatenate([qb * head_lanes[0], qb * head_lanes[1]], axis=0), kt)
    for e in range(2):
        bm = b_rows[e] * LOG2E + dmask
        cm = jnp.max(bm, axis=-1, keepdims=True)
        s = s_heads[e * n_l:(e + 1) * n_l]
        p_heads.append((s * jnp.exp2(bm - cm)).astype(BF16))
        cm_heads.append(cm * LN2)
    cm_pl = jnp.where(head0, cm_heads[0], cm_heads[1])
    v_heads = jnp.concatenate([jnp.where(head0, v, 0.0), jnp.where(head0, 0.0, v)], axis=0).astype(BF16)
    intra = _dot(jnp.concatenate(p_heads, axis=1), jnp.concatenate([v_heads, ones_blk], axis=1))
    w_max = jnp.max(w, axis=0, keepdims=True)
    a_s = jnp.exp(w - w_max)
    upd = jnp.where(bd_mask, _dot(kt, jnp.concatenate([a_s * v, a_s], axis=1).astype(BF16)), 0.0)

    mx = jnp.maximum(m_pl, cm_pl)
    w_intra = jnp.exp(cm_pl - mx)
    w_inter = jnp.exp(m_pl - mx)
    inter = _dot(qb, st.astype(BF16))
    num = w_inter * inter[:, :LANES] + w_intra * intra[:, :LANES]
    den = w_inter * inter[:, LANES:] + w_intra * intra[:, LANES:]
    h = num / jnp.maximum(jnp.abs(den), jnp.exp(-(fc + mx)))
    mw = jnp.maximum(m_pl, w_max)
    a_prev = jnp.exp(m_pl - mw)
    a_new = jnp.exp(w_max - mw)
    st_new = jnp.concatenate([a_prev, a_prev], axis=1) * st + jnp.concatenate([a_new, a_new], axis=1) * upd
    return h, st_new, f_tot + mw


def _mix_c_kernel(zq_ref, zk_ref, zv_ref, zo_ref, xg_ref, rg_ref,
                  cq_ref, ck_ref, cv_ref, co_ref, cxg_ref, crg_ref,
                  wq_ref, wk_ref, bq_ref, bk_ref, cos_ref, sin_ref, ep_ref, tri_ref, tri2_ref, dmask_ref,
                  o_ref, oc_ref, qs_ref, kt_ref):
    s_len = zq_ref.shape[2]
    n_l = C_LEN
    n_chunks = s_len // n_l
    lane = lax.broadcasted_iota(jnp.int32, (n_l, LANES), 1)
    row = lax.broadcasted_iota(jnp.int32, (n_l, LANES), 0)
    k_scale = HEAD_DIM ** -0.5

    def conv_silu(x_ref, t0, total, w_ref_, b_ref_):
        x = x_ref[0, 0, pl.ds(t0, n_l), :]
        if total == n_l:
            prev_row = next_row = jnp.zeros((1, LANES), F32)
        else:
            prev_row = x_ref[0, 0, pl.ds(jnp.maximum(t0 - 1, 0), 1), :] * (t0 > 0).astype(F32)
            next_row = x_ref[0, 0, pl.ds(jnp.minimum(t0 + n_l, total - 1), 1), :] * (t0 + n_l < total).astype(F32)
        x_prev = jnp.where(row == 0, prev_row, pltpu.roll(x, 1, axis=0))
        x_next = jnp.where(row == n_l - 1, next_row, pltpu.roll(x, n_l - 1, axis=0))
        y = x_prev * w_ref_[0, 0:1, :] + x * w_ref_[0, 1:2, :] + x_next * w_ref_[0, 2:3, :] + b_ref_[0]
        return _silu(y)

    def rope(x, t0):
        cos = cos_ref[pl.ds(t0, n_l), :]
        sin = sin_ref[pl.ds(t0, n_l), :]
        first = (lane % (HEAD_DIM // 2)) < (HEAD_DIM // 4)
        partner = jnp.where(first, pltpu.roll(x, LANES - HEAD_DIM // 4, axis=1), pltpu.roll(x, HEAD_DIM // 4, axis=1))
        return x * cos + partner * sin

    def qk_prep(q_in, k_in, t0, total, with_rope):
        q = conv_silu(q_in, t0, total, wq_ref, bq_ref)
        k = conv_silu(k_in, t0, total, wk_ref, bk_ref)
        if with_rope:
            q, k = rope(q, t0), rope(k, t0)
        return q.astype(BF16), (k * k_scale).T.astype(BF16)

    def gate_prep(xg, rg, d):
        if_pl = _dot2_r(xg, ep_ref[0, d])
        i_pl, f_pl = if_pl[:, :LANES], if_pl[:, LANES:]
        fc = _dot2_l(tri_ref[d], f_pl)
        f_row = _dot2_r(rg, tri2_ref[1 - d])
        b_rows = [rg[2 * d + e:2 * d + e + 1, :] - f_row[4 + 2 * d + e:5 + 2 * d + e, :] for e in range(2)]
        return fc, i_pl - fc, b_rows

    head0 = lane < HEAD_DIM
    lane_row = lane[0:1, :]
    head_lanes = [jnp.where((lane_row < HEAD_DIM) == (e == 0), 1.0, 0.0).astype(BF16) for e in range(2)]
    ones_blk = jnp.concatenate([jnp.where(head0, 1.0, 0.0), jnp.where(head0, 0.0, 1.0)], axis=0).astype(BF16)
    bd_r = lax.broadcasted_iota(jnp.int32, (LANES, 2 * LANES), 0)
    bd_c = lax.broadcasted_iota(jnp.int32, (LANES, 2 * LANES), 1)
    bd_mask = (bd_r < HEAD_DIM) == ((bd_c % LANES) < HEAD_DIM)
    consts = (head0, head_lanes, ones_blk, bd_mask)

    cqb, ckt = qk_prep(cq_ref, ck_ref, 0, n_l, False)
    carry, hc = [], []
    for d in range(2):
        fc, w, b_rows = gate_prep(cxg_ref[0, 0], crg_ref[0, 0], d)
        h, st, m_pl = _mlstm_chunk(cqb, ckt, cv_ref[0, 0], fc, w, b_rows, dmask_ref[d],
                                   jnp.zeros((LANES, 2 * LANES), F32), jnp.zeros((1, LANES), F32), d, *consts)
        hc.append(h)
        carry.append((st, m_pl))
    oc_ref[0, 0] = (hc[0] + hc[1]) * _sigmoid(co_ref[0, 0])

    def make_step(second_visit):
        def step(i, carry):
            new = []
            for d in range(2):
                st, m_pl = carry[d]
                c = i if d == 0 else n_chunks - 1 - i
                t0 = pl.multiple_of(c * n_l, n_l)
                rows = pl.ds(t0, n_l)
                if second_visit:
                    qb, kt = qs_ref[rows, :], kt_ref[:, rows]
                else:
                    qb, kt = qk_prep(zq_ref, zk_ref, t0, s_len, True)
                    qs_ref[rows, :] = qb
                    kt_ref[:, rows] = kt
                fc, w, b_rows = gate_prep(xg_ref[0, 0, rows, :], rg_ref[0, 0, :, rows], d)
                h, st, m_pl = _mlstm_chunk(qb, kt, zv_ref[0, 0, rows, :], fc, w, b_rows, dmask_ref[d], st, m_pl, d, *consts)
                if second_visit:
                    o_ref[0, 0, rows, :] = (o_ref[0, 0, rows, :] + h) * _sigmoid(zo_ref[0, 0, rows, :])
                else:
                    o_ref[0, 0, rows, :] = h
                new.append((st, m_pl))
            return tuple(new)
        return step

    carry = lax.fori_loop(0, n_chunks // 2, make_step(False), tuple(carry))
    lax.fori_loop(n_chunks // 2, n_chunks, make_step(True), carry)


def _mix_c_call(z, zg, zc, zcg, wconv, bconv, cos_t, sin_t, ep, tri, tri2, dmask):
    b, _, s, _ = z.shape
    cl = zc.shape[2]
    assert cl == C_LEN and s % (2 * C_LEN) == 0
    once = dict(pipeline_mode=pl.Buffered(1))
    zt = lambda t: pl.BlockSpec((1, 1, s, LANES), lambda i, p: (i, t + p, 0, 0))
    ct = lambda t: pl.BlockSpec((1, 1, cl, LANES), lambda i, p: (i, t + p, 0, 0))
    return pl.pallas_call(
        _mix_c_kernel,
        grid=(b, N_PAIRS),
        in_specs=[zt(T_CQ), zt(T_CK), zt(T_CV), zt(T_CO),
                  pl.BlockSpec((1, 1, s, LANES), lambda i, p: (i, T_CG, 0, 0)),
                  pl.BlockSpec((1, 1, 8, s), lambda i, p: (i, p, 0, 0)),
                  ct(T_CQ), ct(T_CK), ct(T_CV), ct(T_CO),
                  pl.BlockSpec((1, 1, cl, LANES), lambda i, p: (i, T_CG, 0, 0)),
                  pl.BlockSpec((1, 1, 8, cl), lambda i, p: (i, p, 0, 0)),
                  pl.BlockSpec((1, 3, LANES), lambda i, p: (p, 0, 0)),
                  pl.BlockSpec((1, 3, LANES), lambda i, p: (N_PAIRS + p, 0, 0)),
                  pl.BlockSpec((1, 1, LANES), lambda i, p: (p, 0, 0)),
                  pl.BlockSpec((1, 1, LANES), lambda i, p: (N_PAIRS + p, 0, 0)),
                  pl.BlockSpec((s, LANES), lambda i, p: (0, 0), **once),
                  pl.BlockSpec((s, LANES), lambda i, p: (0, 0), **once),
                  pl.BlockSpec((1, 2, 2 * LANES, 2 * LANES), lambda i, p: (p, 0, 0, 0)),
                  pl.BlockSpec((2, C_LEN, C_LEN), lambda i, p: (0, 0, 0), **once),
                  pl.BlockSpec((2, 2 * C_LEN, C_LEN), lambda i, p: (0, 0, 0), **once),
                  pl.BlockSpec((2, C_LEN, C_LEN), lambda i, p: (0, 0, 0), **once)],
        out_specs=[pl.BlockSpec((1, 1, s, LANES), lambda i, p: (i, p, 0, 0)),
                   pl.BlockSpec((1, 1, cl, LANES), lambda i, p: (i, p, 0, 0))],
        out_shape=[jax.ShapeDtypeStruct((b, N_PAIRS, s, LANES), F32),
                   jax.ShapeDtypeStruct((b, N_PAIRS, cl, LANES), F32)],
        scratch_shapes=[pltpu.VMEM((s, LANES), BF16), pltpu.VMEM((LANES, s), BF16)],
        compiler_params=_cparams(("arbitrary", "arbitrary")),
        name="mix_c_mlstm",
    )(z, z, z, z, z, zg.reshape(b, N_PAIRS, 8, s), zc, zc, zc, zc, zc, zcg.reshape(b, N_PAIRS, 8, cl),
      wconv, wconv, bconv, bconv, cos_t, sin_t, ep, tri, tri2, dmask)


def _rope_tables(s):
    t = jnp.arange(s)
    quarter = HEAD_DIM // 4
    inv_freq = ROPE_BASE ** (-jnp.arange(quarter, dtype=F32) / quarter)
    lane = np.arange(LANES) % HEAD_DIM
    use_col = jnp.asarray(lane >= HEAD_DIM // 2)
    second = jnp.asarray((lane % (HEAD_DIM // 2)) >= quarter)
    freq = inv_freq[np.asarray(lane % quarter)]
    pos = jnp.where(use_col[None, :], (t % GRID_W)[:, None], (t // GRID_W)[:, None]).astype(F32)
    ang = pos * freq[None, :]
    return jnp.cos(ang), jnp.where(second[None, :], jnp.sin(ang), -jnp.sin(ang))


def _gate_tables():
    ep = np.zeros((N_PAIRS, 2, 2 * LANES, 2 * LANES), np.float32)
    for p in range(N_PAIRS):
        for k in range(4):
            for ln in range(LANES):
                ep[p, k // 2, 8 * p + 2 * k + ln // HEAD_DIM, (k % 2) * LANES + ln] = 1.0
    ep[:, :, LANES:, :] = ep[:, :, :LANES, :]
    t = np.arange(C_LEN)
    tri = np.stack([(t[None, :] <= t[:, None]), (t[None, :] >= t[:, None])])
    dmask = np.where(tri, 0.0, NEG).astype(np.float32)
    tri = tri.astype(np.float32)
    return (jnp.asarray(ep, BF16), jnp.asarray(tri, BF16), jnp.asarray(np.concatenate([tri, tri], axis=1), BF16),
            jnp.asarray(dmask))


def _post_kernel(x_ref, ya_ref, yb_ref, yc_ref, mod_ref, wout_ref, g_ref, wg_ref, wu_ref, wd_ref, fg_ref, o_ref,
                 *, final, ff_chunk):
    y = jnp.concatenate([ya_ref[0, 0], ya_ref[0, 1], yb_ref[0, 0], yb_ref[0, 1], yb_ref[0, 2],
                         yc_ref[0, 0], yc_ref[0, 1], yc_ref[0, 2]], axis=1).astype(BF16)
    x1 = x_ref[0] + mod_ref[0, 2:3, :] * _dot(y, wout_ref[...])
    hm = _rms_modulate(x1, g_ref[...], mod_ref[0, 3:4, :], mod_ref[0, 4:5, :]).astype(BF16)
    acc = jnp.zeros_like(x1)
    for c in range(wg_ref.shape[1] // ff_chunk):
        cols = slice(c * ff_chunk, (c + 1) * ff_chunk)
        a = _silu(_dot(hm, wg_ref[:, cols])) * _dot(hm, wu_ref[:, cols])
        acc = acc + _dot(a.astype(BF16), wd_ref[cols, :])
    x2 = x1 + mod_ref[0, 5:6, :] * acc
    if final:
        ms = jnp.mean(x2 * x2, axis=-1, keepdims=True)
        x2 = x2 * lax.rsqrt(ms + EPS) * fg_ref[...]
    o_ref[0] = x2


def _post_call(x, ya, yb, yc, mod, wout, g, wg, wu, wd, fg, tm, final):
    b, s, d = x.shape
    dff = wg.shape[1]
    per_sample = mod.shape[0] == b
    mod_map = (lambda i, j: (i, 0, 0)) if per_sample else (lambda i, j: (0, 0, 0))
    const = lambda i, j: (0, 0)
    once = dict(pipeline_mode=pl.Buffered(1))
    yt = lambda n: pl.BlockSpec((1, n, tm, LANES), lambda i, j: (i, 0, j, 0))
    return pl.pallas_call(
        functools.partial(_post_kernel, final=final, ff_chunk=MXU_N),
        grid=(b, s // tm),
        in_specs=[pl.BlockSpec((1, tm, d), lambda i, j: (i, j, 0)),
                  yt(2), yt(N_PAIRS), yt(N_PAIRS),
                  pl.BlockSpec((1, N_MOD, d), mod_map),
                  pl.BlockSpec((d, d), const, **once),
                  pl.BlockSpec((1, d), const),
                  pl.BlockSpec((d, dff), const, **once),
                  pl.BlockSpec((d, dff), const, **once),
                  pl.BlockSpec((dff, d), const, **once),
                  pl.BlockSpec((1, d), const)],
        out_specs=pl.BlockSpec((1, tm, d), lambda i, j: (i, j, 0)),
        out_shape=jax.ShapeDtypeStruct((b, s, d), F32),
        compiler_params=_cparams(("arbitrary", "arbitrary")),
        name="out_proj_swiglu",
    )(x, ya, yb, yc, mod, wout, g, wg, wu, wd, fg)


def _gate_perms():
    col = np.zeros(N_GATES, np.int32)
    row = np.zeros(N_GATES, np.int32)
    for p in range(N_PAIRS):
        for k in range(4):
            for e in range(2):
                col[8 * p + 2 * k + e] = C_HEADS * k + 2 * p + e
                row[8 * p + 4 * (k % 2) + 2 * (k // 2) + e] = C_HEADS * k + 2 * p + e
    return col, row


def kernel(x, c, ctx, c_ctx, w_mod, b_mod, norm1_g, w_in, a_ln_g, a_ln_b, a_ws, a_bs, b_rpb,
           c_conv_w, c_conv_b, c_gate_b, w_out, norm2_g, w_gate, w_up, w_down, final_g):
    b, s, d = x.shape
    cl = ctx.shape[1]
    depth = w_mod.shape[0]
    n_main = T_CG * LANES
    tm = 512
    perm_c, perm_r = _gate_perms()

    pad = (-(b + 1)) % 8
    cs = jnp.concatenate([c, c_ctx[None, :], jnp.zeros((pad, d), F32)], axis=0)
    mod_all = _mod_call(cs, w_mod.astype(BF16), b_mod)

    cos_t, sin_t = _rope_tables(s)
    ep, tri, tri2, dmask = _gate_tables()
    avg = np.kron(np.eye(2), np.full((HEAD_DIM, HEAD_DIM), 1.0 / HEAD_DIM))
    avg = jnp.asarray(np.concatenate([avg, avg], axis=0), BF16)

    xc = ctx
    for l in range(depth):
        last = l == depth - 1
        mod = mod_all[l, :b].reshape(b, N_MOD, d)
        mod_c = mod_all[l, b:b + 1].reshape(1, N_MOD, d)

        w_gates = w_in[l][:, n_main:]
        w_main = jnp.concatenate([w_in[l][:, :n_main], w_gates[:, perm_c], jnp.zeros((d, LANES - N_GATES), F32)], axis=1)
        w_main = w_main.astype(BF16)
        wgt = w_gates[:, perm_r].T.astype(BF16)
        gb = c_gate_b[l].reshape(-1)
        gbc = jnp.concatenate([gb[perm_c], jnp.zeros((LANES - N_GATES,), F32)])[None, :]
        gbt = jnp.broadcast_to(gb[perm_r][:, None], (N_GATES, tm))
        g1 = norm1_g[l][None, :]

        z, zg = _inproj_call(x, mod, g1, w_main, wgt, gbc, gbt, tm)
        zc, zcg = _inproj_call(xc, mod_c, g1, w_main, wgt, gbc, gbt, cl)

        lng = a_ln_g[l].reshape(2, 1, LANES)
        lnb = a_ln_b[l].reshape(2, 1, LANES)
        ws = a_ws[l].astype(BF16)
        bsx = jnp.repeat(a_bs[l].reshape(2, 2, A_CHUNK).transpose(0, 2, 1), HEAD_DIM, axis=2)
        ya = _mix_a_call(z, lng, lnb, ws, bsx, avg, tm)

        yb = _mix_b_call(z, zc, _natten_bias(b_rpb[l] * LOG2E, s // GRID_W))

        wconv = c_conv_w[l].reshape(3, 2 * N_PAIRS, LANES).transpose(1, 0, 2)
        bconv = c_conv_b[l].reshape(2 * N_PAIRS, 1, LANES)
        yc, ycc = _mix_c_call(z, zg, zc, zcg, wconv, bconv, cos_t, sin_t, ep, tri, tri2, dmask)

        weights = (w_out[l].astype(BF16), norm2_g[l][None, :], w_gate[l].astype(BF16), w_up[l].astype(BF16),
                   w_down[l].astype(BF16), final_g[None, :])
        x = _post_call(x, ya, yb, yc, mod, *weights, tm, last)
        if not last:
            yac = _mix_a_call(zc, lng, lnb, ws, bsx, avg, cl)
            ybc = _mix_b_ctx_call(zc)
            xc = _post_call(xc, yac, ybc, ycc, mod_c, *weights, cl, False)
    return x
```

```python
import functools

import numpy as np
import jax
import jax.numpy as jnp
from jax import lax
from jax.experimental import pallas as pl
from jax.experimental.pallas import tpu as pltpu

F32 = jnp.float32
BF16 = jnp.bfloat16

LANES = 128
MXU_N = 256
HEAD_DIM = 64
GRID_W = 64
A_GROUPS = 4
A_CHUNK = 128
B_HEADS = 6
C_HEADS = 6
WIN_ROWS = 8
WIN_COLS = 16
ROPE_BASE = 10000.0
EPS = 1e-6
N_MOD = 6
NEG = -1e30
LOG2E = float(np.log2(np.e))
LN2 = float(np.log(2.0))

T_AU, T_AV, T_BQ, T_BK, T_BV, T_CQ, T_CK, T_CV, T_CO = 0, 2, 4, 7, 10, 13, 16, 19, 22
N_TILES = 25
N_PAIRS = 3
N_GATES = 4 * C_HEADS
Q_ROWS = 4
K_ROWS = Q_ROWS + WIN_ROWS
C_LEN = 256

VMEM_LIMIT = 56 * 1024 * 1024


def _cparams(sem):
    return pltpu.CompilerParams(dimension_semantics=sem, vmem_limit_bytes=VMEM_LIMIT)


def _dot(a, b):
    return jnp.dot(a, b, preferred_element_type=F32)


def _dot_nt(a, b):
    return lax.dot_general(a, b, (((1,), (1,)), ((), ())), preferred_element_type=F32)


def _split2(x):
    h = x.astype(BF16)
    return h, (x - h.astype(F32)).astype(BF16)


def _dot2_r(x, sel2):
    h, l = _split2(x)
    return _dot(jnp.concatenate([h, l], axis=1), sel2)


def _stack3(x):
    h = x.astype(BF16).astype(F32)
    r = x - h
    m = r.astype(BF16).astype(F32)
    return jnp.concatenate([h, m, r - m], axis=0).astype(BF16)


def _silu(x):
    h = 0.5 * x
    return h + h * jnp.tanh(h)


def _sigmoid(x):
    return 0.5 + 0.5 * jnp.tanh(0.5 * x)


def _gelu_tanh(x):
    return 0.5 * x * (1.0 + jnp.tanh(np.sqrt(2.0 / np.pi).astype(np.float32) * (x + 0.044715 * (x * x * x))))


def _log_sigmoid(x):
    return jnp.minimum(x, 0.0) - jnp.log1p(jnp.exp(-jnp.abs(x)))


def _rms_modulate(x, g, shift, scale):
    ms = jnp.mean(x * x, axis=-1, keepdims=True)
    return (x * lax.rsqrt(ms + EPS) * g) * (1.0 + scale) + shift


def _mod_kernel(c_ref, w_ref, b_ref, o_ref):
    o_ref[0] = _dot(_silu(c_ref[...]).astype(BF16), w_ref[0]) + b_ref[0, 0]


def _mod_call(cs, w_mod, b_mod):
    depth, d, nd = w_mod.shape
    rows = cs.shape[0]
    return pl.pallas_call(
        _mod_kernel,
        grid=(depth, nd // d),
        in_specs=[pl.BlockSpec((rows, d), lambda l, n: (0, 0)),
                  pl.BlockSpec((1, d, d), lambda l, n: (l, 0, n)),
                  pl.BlockSpec((1, 1, 1, d), lambda l, n: (l, n, 0, 0))],
        out_specs=pl.BlockSpec((1, rows, d), lambda l, n: (l, 0, n)),
        out_shape=jax.ShapeDtypeStruct((depth, rows, nd), F32),
        compiler_params=_cparams(("arbitrary", "arbitrary")),
        name="adaln_mod",
    )(cs, w_mod, b_mod.reshape(depth, nd // d, 1, d))


def _inproj_kernel(x_ref, mod_ref, g_ref, w_ref, gbt_ref, z_ref, zgt_ref):
    xm = _rms_modulate(x_ref[0], g_ref[...], mod_ref[0, 0:1, :], mod_ref[0, 1:2, :])
    xb = xm.astype(BF16)
    for j in range((N_TILES + 1) // 2):
        r = _dot(xb, w_ref[:, j * MXU_N:(j + 1) * MXU_N])
        z_ref[0, 2 * j] = r[:, :LANES]
        if 2 * j + 1 < N_TILES:
            z_ref[0, 2 * j + 1] = r[:, LANES:]
    g = r[:, LANES:].T[:N_GATES, :] + gbt_ref[...]
    sub = lax.broadcasted_iota(jnp.int32, g.shape, 0)
    zgt_ref[0] = jnp.where(sub % 8 >= 4, _log_sigmoid(g), g)


def _inproj_call(x, mod, g, w, gbt, tm):
    b, s, d = x.shape
    per_sample = mod.shape[0] == b
    mod_map = (lambda i, j: (i, 0, 0)) if per_sample else (lambda i, j: (0, 0, 0))
    const = lambda i, j: (0, 0)
    return pl.pallas_call(
        _inproj_kernel,
        grid=(b, s // tm),
        in_specs=[pl.BlockSpec((1, tm, d), lambda i, j: (i, j, 0)),
                  pl.BlockSpec((1, N_MOD, d), mod_map),
                  pl.BlockSpec((1, d), const),
                  pl.BlockSpec((d, (N_TILES + 1) * LANES), const),
                  pl.BlockSpec((N_GATES, tm), const)],
        out_specs=[pl.BlockSpec((1, N_TILES, tm, LANES), lambda i, j: (i, 0, j, 0)),
                   pl.BlockSpec((1, N_GATES, tm), lambda i, j: (i, 0, j))],
        out_shape=[jax.ShapeDtypeStruct((b, N_TILES, s, LANES), F32),
                   jax.ShapeDtypeStruct((b, N_GATES, s), F32)],
        compiler_params=_cparams(("arbitrary", "arbitrary")),
        name="in_proj",
    )(x, mod, g, w, gbt[:, :tm])


def _mix_a_kernel(u_ref, v_ref, lng_ref, lnb_ref, ws_ref, bsx_ref, avg_ref, o_ref, *, n_chunks):
    lane = lax.broadcasted_iota(jnp.int32, (A_CHUNK, LANES), 1)
    avg = avg_ref[...]
    for j in range(2):
        v = _gelu_tanh(v_ref[0, j])
        mu = _dot2_r(v, avg)
        dv = v - mu
        var = _dot2_r(dv * dv, avg)
        vn = (dv * lax.rsqrt(var + EPS) * lng_ref[j] + lnb_ref[j]).astype(BF16)
        for c in range(n_chunks):
            rows = slice(c * A_CHUNK, (c + 1) * A_CHUNK)
            vc = vn[rows]
            s = jnp.where(lane < HEAD_DIM, _dot(ws_ref[2 * j], vc), _dot(ws_ref[2 * j + 1], vc)) + bsx_ref[j]
            o_ref[0, j, rows, :] = _gelu_tanh(u_ref[0, j, rows, :]) * s


def _mix_a_call(z, lng, lnb, ws, bsx, avg, t):
    b, _, s, _ = z.shape
    c3 = lambda i, j: (0, 0, 0)
    return pl.pallas_call(
        functools.partial(_mix_a_kernel, n_chunks=t // A_CHUNK),
        grid=(b, s // t),
        in_specs=[pl.BlockSpec((1, 2, t, LANES), lambda i, j: (i, T_AU // 2, j, 0)),
                  pl.BlockSpec((1, 2, t, LANES), lambda i, j: (i, T_AV // 2, j, 0)),
                  pl.BlockSpec((2, 1, LANES), c3),
                  pl.BlockSpec((2, 1, LANES), c3),
                  pl.BlockSpec((A_GROUPS, A_CHUNK, A_CHUNK), c3),
                  pl.BlockSpec((2, A_CHUNK, LANES), c3),
                  pl.BlockSpec((2 * LANES, LANES), lambda i, j: (0, 0))],
        out_specs=pl.BlockSpec((1, 2, t, LANES), lambda i, j: (i, 0, j, 0)),
        out_shape=jax.ShapeDtypeStruct((b, 2, s, LANES), F32),
        compiler_params=_cparams(("arbitrary", "arbitrary")),
        name="mix_a_gmlp",
    )(z, z, lng, lnb, ws, bsx, avg)


def _mix_b_kernel(q_ref, k_ref, v_ref, kc_ref, vc_ref, bias_ref, o_ref, kb_ref, vb_ref, *, n_rows):
    n_blocks = n_rows // Q_ROWS
    nq = Q_ROWS * GRID_W
    nk = K_ROWS * GRID_W
    kb_ref[...] = k_ref[0, 0].astype(BF16)
    vb_ref[...] = v_ref[0, 0].astype(BF16)
    kc = kc_ref[0, 0].astype(BF16)
    vc = vc_ref[0, 0].astype(BF16)
    lane = lax.broadcasted_iota(jnp.int32, (nq, LANES), 1)

    def block(rb, carry):
        q_start = pl.multiple_of(rb * nq, nq)
        k_start = pl.multiple_of(jnp.clip(rb * Q_ROWS - WIN_ROWS // 2, 0, n_rows - K_ROWS) * GRID_W, nq)
        cls = jnp.where(rb == 0, 0, jnp.where(rb == n_blocks - 1, 2, 1))
        q = q_ref[0, 0, pl.ds(q_start, nq), :] * (HEAD_DIM ** -0.5 * LOG2E)
        kw = kb_ref[pl.ds(k_start, nk), :]
        vw = vb_ref[pl.ds(k_start, nk), :]
        outs = []
        for e in range(2):
            qe = jnp.where((lane < HEAD_DIM) == (e == 0), q, 0.0).astype(BF16)
            s_w = _dot_nt(qe, kw) + bias_ref[0, e, cls]
            s_c = _dot_nt(qe, kc)
            m = jnp.maximum(jnp.max(s_w, axis=-1, keepdims=True), jnp.max(s_c, axis=-1, keepdims=True))
            p_w = jnp.exp2(s_w - m)
            p_c = jnp.exp2(s_c - m)
            den = jnp.sum(p_w, axis=-1, keepdims=True) + jnp.sum(p_c, axis=-1, keepdims=True)
            o = _dot(p_w.astype(BF16), vw) + _dot(p_c.astype(BF16), vc)
            outs.append(o / den)
        o_ref[0, 0, pl.ds(q_start, nq), :] = jnp.where(lane < HEAD_DIM, outs[0], outs[1])
        return carry

    lax.fori_loop(0, n_blocks, block, 0, unroll=2)


def _mix_b_call(z, zc, bias):
    b, _, s, _ = z.shape
    cl = zc.shape[2]
    n_rows = s // GRID_W
    assert n_rows % (2 * Q_ROWS) == 0 and n_rows >= K_ROWS
    zt = lambda t: pl.BlockSpec((1, 1, s, LANES), lambda p, i: (i, t + p, 0, 0))
    ct = lambda t: pl.BlockSpec((1, 1, cl, LANES), lambda p, i: (i, t + p, 0, 0))
    return pl.pallas_call(
        functools.partial(_mix_b_kernel, n_rows=n_rows),
        grid=(N_PAIRS, b),
        in_specs=[zt(T_BQ), zt(T_BK), zt(T_BV), ct(T_BK), ct(T_BV),
                  pl.BlockSpec((1, 2, 3, Q_ROWS * GRID_W, K_ROWS * GRID_W), lambda p, i: (p, 0, 0, 0, 0))],
        out_specs=pl.BlockSpec((1, 1, s, LANES), lambda p, i: (i, p, 0, 0)),
        out_shape=jax.ShapeDtypeStruct((b, N_PAIRS, s, LANES), F32),
        scratch_shapes=[pltpu.VMEM((s, LANES), BF16), pltpu.VMEM((s, LANES), BF16)],
        compiler_params=_cparams(("arbitrary", "arbitrary")),
        name="mix_b_natten",
    )(z, z, z, zc, zc, bias)


def _mix_b_ctx_kernel(q_ref, k_ref, v_ref, o_ref):
    q = q_ref[0, 0] * (HEAD_DIM ** -0.5)
    k = k_ref[0, 0].astype(BF16)
    v = v_ref[0, 0].astype(BF16)
    lane = lax.broadcasted_iota(jnp.int32, q.shape, 1)
    outs = []
    for e in range(2):
        qe = jnp.where((lane < HEAD_DIM) == (e == 0), q, 0.0).astype(BF16)
        s = _dot_nt(qe, k)
        p = jnp.exp(s - jnp.max(s, axis=-1, keepdims=True))
        outs.append(_dot(p.astype(BF16), v) / jnp.sum(p, axis=-1, keepdims=True))
    o_ref[0, 0] = jnp.where(lane < HEAD_DIM, outs[0], outs[1])


def _mix_b_ctx_call(zc):
    b, _, cl, _ = zc.shape
    blk = lambda t: pl.BlockSpec((1, 1, cl, LANES), lambda i, p: (i, t + p, 0, 0))
    return pl.pallas_call(
        _mix_b_ctx_kernel,
        grid=(b, N_PAIRS),
        in_specs=[blk(T_BQ), blk(T_BK), blk(T_BV)],
        out_specs=blk(0),
        out_shape=jax.ShapeDtypeStruct((b, N_PAIRS, cl, LANES), F32),
        compiler_params=_cparams(("arbitrary", "arbitrary")),
        name="mix_b_ctx",
    )(zc, zc, zc)


def _natten_bias(rpb, n_rows):
    h = rpb.shape[0]
    padded = jnp.pad(rpb, ((0, 0), (0, 0), (GRID_W, GRID_W)))
    toe = jnp.stack([padded[:, :, GRID_W + WIN_COLS - 1 - c:2 * GRID_W + WIN_COLS - 1 - c] for c in range(GRID_W)], axis=2)
    c = np.arange(GRID_W)[:, None]
    kc = np.arange(GRID_W)[None, :]
    cs = np.clip(c - WIN_COLS // 2, 0, GRID_W - WIN_COLS)
    col_ok = (kc >= cs) & (kc < cs + WIN_COLS)
    toe = jnp.where(col_ok[None, None], toe, NEG)
    neg_block = jnp.full((h, GRID_W, GRID_W), NEG, F32)
    classes = []
    for r0 in (0, Q_ROWS, n_rows - Q_ROWS):
        k0 = int(np.clip(r0 - WIN_ROWS // 2, 0, n_rows - K_ROWS))
        q_rows = []
        for i in range(Q_ROWS):
            r = r0 + i
            rs = int(np.clip(r - WIN_ROWS // 2, 0, n_rows - WIN_ROWS))
            blocks = []
            for j in range(K_ROWS):
                kr = k0 + j
                blocks.append(toe[:, kr - r + WIN_ROWS - 1] if rs <= kr < rs + WIN_ROWS else neg_block)
            q_rows.append(jnp.concatenate(blocks, axis=-1))
        classes.append(jnp.concatenate(q_rows, axis=1))
    bias = jnp.stack(classes, axis=1)
    return bias.reshape(N_PAIRS, 2, 3, Q_ROWS * GRID_W, K_ROWS * GRID_W)


def _mlstm_chunk(qb, kt, v, fc, w, b_rows, dmask, st, m_pl, direction, head0, head_lanes, ones_blk, bd_mask):
    n_l = qb.shape[0]
    f_tot = fc[n_l - 1:n_l, :] if direction == 0 else fc[0:1, :]
    p_heads, cm_heads = [], []
    s_heads = _dot(jnp.concatenate([qb * head_lanes[0], qb * head_lanes[1]], axis=0), kt)
    for e in range(2):
        bm = b_rows[e] * LOG2E + dmask
        cm = jnp.max(bm, axis=-1, keepdims=True)
        s = s_heads[e * n_l:(e + 1) * n_l]
        p_heads.append((s * jnp.exp2(bm - cm)).astype(BF16))
        cm_heads.append(cm * LN2)
    cm_pl = jnp.where(head0, cm_heads[0], cm_heads[1])
    v_heads = jnp.concatenate([jnp.where(head0, v, 0.0), jnp.where(head0, 0.0, v)], axis=0).astype(BF16)
    intra = _dot(jnp.concatenate(p_heads, axis=1), jnp.concatenate([v_heads, ones_blk], axis=1))
    w_max = jnp.max(w, axis=0, keepdims=True)
    a_s = jnp.exp(w - w_max)
    upd = jnp.where(bd_mask, _dot(kt, jnp.concatenate([a_s * v, a_s], axis=1).astype(BF16)), 0.0)

    mx = jnp.maximum(m_pl, cm_pl)
    w_intra = jnp.exp(cm_pl - mx)
    w_inter = jnp.exp(m_pl - mx)
    inter = _dot(qb, st.astype(BF16))
    num = w_inter * inter[:, :LANES] + w_intra * intra[:, :LANES]
    den = w_inter * inter[:, LANES:] + w_intra * intra[:, LANES:]
    h = num / jnp.maximum(jnp.abs(den), jnp.exp(-(fc + mx)))
    mw = jnp.maximum(m_pl, w_max)
    a_prev = jnp.exp(m_pl - mw)
    a_new = jnp.exp(w_max - mw)
    st_new = jnp.concatenate([a_prev, a_prev], axis=1) * st + jnp.concatenate([a_new, a_new], axis=1) * upd
    return h, st_new, f_tot + mw


def _mix_c_kernel(zq_ref, zk_ref, zv_ref, zo_ref, rg_ref, cq_ref, ck_ref, cv_ref, co_ref, crg_ref,
                  wq_ref, wk_ref, bq_ref, bk_ref, cos_ref, sin_ref, esel_ref, tri_ref, dmask_ref,
                  o_ref, oc_ref, qs_ref, kt_ref, fw_ref, br_ref):
    s_len = zq_ref.shape[2]
    n_l = C_LEN
    n_chunks = s_len // n_l
    lane = lax.broadcasted_iota(jnp.int32, (n_l, LANES), 1)
    row = lax.broadcasted_iota(jnp.int32, (n_l, LANES), 0)
    k_scale = HEAD_DIM ** -0.5

    def conv_silu(x_ref, t0, total, w_ref_, b_ref_):
        x = x_ref[0, 0, pl.ds(t0, n_l), :]
        if total == n_l:
            prev_row = next_row = jnp.zeros((1, LANES), F32)
        else:
            prev_row = x_ref[0, 0, pl.ds(jnp.maximum(t0 - 1, 0), 1), :] * (t0 > 0).astype(F32)
            next_row = x_ref[0, 0, pl.ds(jnp.minimum(t0 + n_l, total - 1), 1), :] * (t0 + n_l < total).astype(F32)
        x_prev = jnp.where(row == 0, prev_row, pltpu.roll(x, 1, axis=0))
        x_next = jnp.where(row == n_l - 1, next_row, pltpu.roll(x, n_l - 1, axis=0))
        y = x_prev * w_ref_[0, 0:1, :] + x * w_ref_[0, 1:2, :] + x_next * w_ref_[0, 2:3, :] + b_ref_[0]
        return _silu(y)

    def rope(x, t0):
        cos = cos_ref[pl.ds(t0, n_l), :]
        sin = sin_ref[pl.ds(t0, n_l), :]
        first = (lane % (HEAD_DIM // 2)) < (HEAD_DIM // 4)
        partner = jnp.where(first, pltpu.roll(x, LANES - HEAD_DIM // 4, axis=1), pltpu.roll(x, HEAD_DIM // 4, axis=1))
        return x * cos + partner * sin

    def qk_prep(q_in, k_in, t0, total, with_rope):
        q = conv_silu(q_in, t0, total, wq_ref, bq_ref)
        k = conv_silu(k_in, t0, total, wk_ref, bk_ref)
        if with_rope:
            q, k = rope(q, t0), rope(k, t0)
        return q.astype(BF16), (k * k_scale).T.astype(BF16)

    def gate_prep(rg, d):
        n_g = rg.shape[0]
        f3 = _dot(_stack3(rg), tri_ref[1 - d])
        f_row = f3[:n_g] + f3[n_g:2 * n_g] + f3[2 * n_g:]
        b_rows = [rg[2 * d + e:2 * d + e + 1, :] - f_row[4 + 2 * d + e:5 + 2 * d + e, :] for e in range(2)]
        fw = lax.dot_general(_stack3(jnp.concatenate([f_row, rg], axis=0)), esel_ref[d], (((0,), (0,)), ((), ())),
                             preferred_element_type=F32)
        return fw[:, :LANES], fw[:, LANES:], b_rows

    head0 = lane < HEAD_DIM
    lane_row = lane[0:1, :]
    head_lanes = [jnp.where((lane_row < HEAD_DIM) == (e == 0), 1.0, 0.0).astype(BF16) for e in range(2)]
    ones_blk = jnp.concatenate([jnp.where(head0, 1.0, 0.0), jnp.where(head0, 0.0, 1.0)], axis=0).astype(BF16)
    bd_r = lax.broadcasted_iota(jnp.int32, (LANES, 2 * LANES), 0)
    bd_c = lax.broadcasted_iota(jnp.int32, (LANES, 2 * LANES), 1)
    bd_mask = (bd_r < HEAD_DIM) == ((bd_c % LANES) < HEAD_DIM)
    consts = (head0, head_lanes, ones_blk, bd_mask)

    cqb, ckt = qk_prep(cq_ref, ck_ref, 0, n_l, False)
    carry, hc = [], []
    for d in range(2):
        fc, w, b_rows = gate_prep(crg_ref[0, 0], d)
        h, st, m_pl = _mlstm_chunk(cqb, ckt, cv_ref[0, 0], fc, w, b_rows, dmask_ref[d],
                                   jnp.zeros((LANES, 2 * LANES), F32), jnp.zeros((1, LANES), F32), d, *consts)
        hc.append(h)
        carry.append((st, m_pl))
    oc_ref[0, 0] = (hc[0] + hc[1]) * _sigmoid(co_ref[0, 0])

    def prep(c, carry):
        t0 = pl.multiple_of(c * n_l, n_l)
        rows = pl.ds(t0, n_l)
        qs_ref[rows, :], kt_ref[:, rows] = qk_prep(zq_ref, zk_ref, t0, s_len, True)
        rg = rg_ref[0, 0, :, rows]
        for d in range(2):
            fc, w, b_rows = gate_prep(rg, d)
            fw_ref[d, rows, :] = jnp.concatenate([fc, w], axis=1)
            for e in range(2):
                br_ref[2 * d + e:2 * d + e + 1, rows] = b_rows[e]
        return carry

    lax.fori_loop(0, n_chunks, prep, 0, unroll=4)

    def make_step(second_visit):
        def step(i, carry):
            new = []
            for d in range(2):
                st, m_pl = carry[d]
                c = i if d == 0 else n_chunks - 1 - i
                rows = pl.ds(pl.multiple_of(c * n_l, n_l), n_l)
                fw = fw_ref[d, rows, :]
                br = br_ref[:, rows]
                b_rows = [br[2 * d + e:2 * d + e + 1, :] for e in range(2)]
                h, st, m_pl = _mlstm_chunk(qs_ref[rows, :], kt_ref[:, rows], zv_ref[0, 0, rows, :], fw[:, :LANES],
                                           fw[:, LANES:], b_rows, dmask_ref[d], st, m_pl, d, *consts)
                if second_visit:
                    o_ref[0, 0, rows, :] = (o_ref[0, 0, rows, :] + h) * _sigmoid(zo_ref[0, 0, rows, :])
                else:
                    o_ref[0, 0, rows, :] = h
                new.append((st, m_pl))
            return tuple(new)
        return step

    carry = lax.fori_loop(0, n_chunks // 2, make_step(False), tuple(carry))
    lax.fori_loop(n_chunks // 2, n_chunks, make_step(True), carry)


def _mix_c_call(z, zg, zc, zcg, wconv, bconv, cos_t, sin_t, esel, tri, dmask):
    b, _, s, _ = z.shape
    cl = zc.shape[2]
    assert cl == C_LEN and s % (2 * C_LEN) == 0
    once = dict(pipeline_mode=pl.Buffered(1))
    zt = lambda t: pl.BlockSpec((1, 1, s, LANES), lambda i, p: (i, t + p, 0, 0))
    ct = lambda t: pl.BlockSpec((1, 1, cl, LANES), lambda i, p: (i, t + p, 0, 0))
    return pl.pallas_call(
        _mix_c_kernel,
        grid=(b, N_PAIRS),
        in_specs=[zt(T_CQ), zt(T_CK), zt(T_CV), zt(T_CO),
                  pl.BlockSpec((1, 1, 8, s), lambda i, p: (i, p, 0, 0)),
                  ct(T_CQ), ct(T_CK), ct(T_CV), ct(T_CO),
                  pl.BlockSpec((1, 1, 8, cl), lambda i, p: (i, p, 0, 0)),
                  pl.BlockSpec((1, 3, LANES), lambda i, p: (p, 0, 0)),
                  pl.BlockSpec((1, 3, LANES), lambda i, p: (N_PAIRS + p, 0, 0)),
                  pl.BlockSpec((1, 1, LANES), lambda i, p: (p, 0, 0)),
                  pl.BlockSpec((1, 1, LANES), lambda i, p: (N_PAIRS + p, 0, 0)),
                  pl.BlockSpec((s, LANES), lambda i, p: (0, 0), **once),
                  pl.BlockSpec((s, LANES), lambda i, p: (0, 0), **once),
                  pl.BlockSpec((2, 48, 2 * LANES), lambda i, p: (0, 0, 0), **once),
                  pl.BlockSpec((2, C_LEN, C_LEN), lambda i, p: (0, 0, 0), **once),
                  pl.BlockSpec((2, C_LEN, C_LEN), lambda i, p: (0, 0, 0), **once)],
        out_specs=[pl.BlockSpec((1, 1, s, LANES), lambda i, p: (i, p, 0, 0)),
                   pl.BlockSpec((1, 1, cl, LANES), lambda i, p: (i, p, 0, 0))],
        out_shape=[jax.ShapeDtypeStruct((b, N_PAIRS, s, LANES), F32),
                   jax.ShapeDtypeStruct((b, N_PAIRS, cl, LANES), F32)],
        scratch_shapes=[pltpu.VMEM((s, LANES), BF16), pltpu.VMEM((LANES, s), BF16),
                        pltpu.VMEM((2, s, 2 * LANES), F32), pltpu.VMEM((8, s), F32)],
        compiler_params=_cparams(("arbitrary", "arbitrary")),
        name="mix_c_mlstm",
    )(z, z, z, z, zg.reshape(b, N_PAIRS, 8, s), zc, zc, zc, zc, zcg.reshape(b, N_PAIRS, 8, cl),
      wconv, wconv, bconv, bconv, cos_t, sin_t, esel, tri, dmask)


def _rope_tables(s):
    t = jnp.arange(s)
    quarter = HEAD_DIM // 4
    inv_freq = ROPE_BASE ** (-jnp.arange(quarter, dtype=F32) / quarter)
    lane = np.arange(LANES) % HEAD_DIM
    use_col = jnp.asarray(lane >= HEAD_DIM // 2)
    second = jnp.asarray((lane % (HEAD_DIM // 2)) >= quarter)
    freq = inv_freq[np.asarray(lane % quarter)]
    pos = jnp.where(use_col[None, :], (t % GRID_W)[:, None], (t // GRID_W)[:, None]).astype(F32)
    ang = pos * freq[None, :]
    return jnp.cos(ang), jnp.where(second[None, :], jnp.sin(ang), -jnp.sin(ang))


def _gate_tables():
    esel = np.zeros((2, 48, 2 * LANES), np.float32)
    for d in range(2):
        for ln in range(LANES):
            e = ln // HEAD_DIM
            esel[d, 4 + 2 * d + e, ln] = 1.0
            esel[d, 4 + 2 * d + e, LANES + ln] = -1.0
            esel[d, 8 + 2 * d + e, LANES + ln] = 1.0
    esel[:, 16:32, :] = esel[:, :16, :]
    esel[:, 32:, :] = esel[:, :16, :]
    t = np.arange(C_LEN)
    tri = np.stack([(t[None, :] <= t[:, None]), (t[None, :] >= t[:, None])])
    dmask = np.where(tri, 0.0, NEG).astype(np.float32)
    return jnp.asarray(esel, BF16), jnp.asarray(tri.astype(np.float32), BF16), jnp.asarray(dmask)


def _post_kernel(x_ref, ya_ref, yb_ref, yc_ref, mod_ref, wout_ref, g_ref, wg_ref, wu_ref, wd_ref, fg_ref, o_ref,
                 *, final, ff_chunk):
    y = jnp.concatenate([ya_ref[0, 0], ya_ref[0, 1], yb_ref[0, 0], yb_ref[0, 1], yb_ref[0, 2],
                         yc_ref[0, 0], yc_ref[0, 1], yc_ref[0, 2]], axis=1).astype(BF16)
    x1 = x_ref[0] + mod_ref[0, 2:3, :] * _dot(y, wout_ref[...])
    hm = _rms_modulate(x1, g_ref[...], mod_ref[0, 3:4, :], mod_ref[0, 4:5, :]).astype(BF16)
    acc = jnp.zeros_like(x1)
    for c in range(wg_ref.shape[1] // ff_chunk):
        cols = slice(c * ff_chunk, (c + 1) * ff_chunk)
        a = _silu(_dot(hm, wg_ref[:, cols])) * _dot(hm, wu_ref[:, cols])
        acc = acc + _dot(a.astype(BF16), wd_ref[cols, :])
    x2 = x1 + mod_ref[0, 5:6, :] * acc
    if final:
        ms = jnp.mean(x2 * x2, axis=-1, keepdims=True)
        x2 = x2 * lax.rsqrt(ms + EPS) * fg_ref[...]
    o_ref[0] = x2


def _post_call(x, ya, yb, yc, mod, wout, g, wg, wu, wd, fg, tm, final):
    b, s, d = x.shape
    dff = wg.shape[1]
    per_sample = mod.shape[0] == b
    mod_map = (lambda i, j: (i, 0, 0)) if per_sample else (lambda i, j: (0, 0, 0))
    const = lambda i, j: (0, 0)
    once = dict(pipeline_mode=pl.Buffered(1))
    yt = lambda n: pl.BlockSpec((1, n, tm, LANES), lambda i, j: (i, 0, j, 0))
    return pl.pallas_call(
        functools.partial(_post_kernel, final=final, ff_chunk=MXU_N),
        grid=(b, s // tm),
        in_specs=[pl.BlockSpec((1, tm, d), lambda i, j: (i, j, 0)),
                  yt(2), yt(N_PAIRS), yt(N_PAIRS),
                  pl.BlockSpec((1, N_MOD, d), mod_map),
                  pl.BlockSpec((d, d), const, **once),
                  pl.BlockSpec((1, d), const),
                  pl.BlockSpec((d, dff), const, **once),
                  pl.BlockSpec((d, dff), const, **once),
                  pl.BlockSpec((dff, d), const, **once),
                  pl.BlockSpec((1, d), const)],
        out_specs=pl.BlockSpec((1, tm, d), lambda i, j: (i, j, 0)),
        out_shape=jax.ShapeDtypeStruct((b, s, d), F32),
        compiler_params=_cparams(("arbitrary", "arbitrary")),
        name="out_proj_swiglu",
    )(x, ya, yb, yc, mod, wout, g, wg, wu, wd, fg)


def _gate_perm():
    row = np.zeros(N_GATES, np.int32)
    for p in range(N_PAIRS):
        for k in range(4):
            for e in range(2):
                row[8 * p + 4 * (k % 2) + 2 * (k // 2) + e] = C_HEADS * k + 2 * p + e
    return row


def kernel(x, c, ctx, c_ctx, w_mod, b_mod, norm1_g, w_in, a_ln_g, a_ln_b, a_ws, a_bs, b_rpb,
           c_conv_w, c_conv_b, c_gate_b, w_out, norm2_g, w_gate, w_up, w_down, final_g):
    b, s, d = x.shape
    cl = ctx.shape[1]
    depth = w_mod.shape[0]
    n_main = N_TILES * LANES
    tm = 512
    perm = _gate_perm()

    pad = (-(b + 1)) % 8
    cs = jnp.concatenate([c, c_ctx[None, :], jnp.zeros((pad, d), F32)], axis=0)
    mod_all = _mod_call(cs, w_mod.astype(BF16), b_mod)

    cos_t, sin_t = _rope_tables(s)
    esel, tri, dmask = _gate_tables()
    avg = np.kron(np.eye(2), np.full((HEAD_DIM, HEAD_DIM), 1.0 / HEAD_DIM))
    avg = jnp.asarray(np.concatenate([avg, avg], axis=0), BF16)

    xc = ctx
    for l in range(depth):
        last = l == depth - 1
        mod = mod_all[l, :b].reshape(b, N_MOD, d)
        mod_c = mod_all[l, b:b + 1].reshape(1, N_MOD, d)

        w_main = jnp.concatenate([w_in[l][:, :n_main], w_in[l][:, n_main:][:, perm],
                                  jnp.zeros((d, LANES - N_GATES), F32)], axis=1).astype(BF16)
        gbt = jnp.broadcast_to(c_gate_b[l].reshape(-1)[perm][:, None], (N_GATES, tm))
        g1 = norm1_g[l][None, :]

        z, zg = _inproj_call(x, mod, g1, w_main, gbt, tm)
        zc, zcg = _inproj_call(xc, mod_c, g1, w_main, gbt, cl)

        lng = a_ln_g[l].reshape(2, 1, LANES)
        lnb = a_ln_b[l].reshape(2, 1, LANES)
        ws = a_ws[l].astype(BF16)
        bsx = jnp.repeat(a_bs[l].reshape(2, 2, A_CHUNK).transpose(0, 2, 1), HEAD_DIM, axis=2)
        ya = _mix_a_call(z, lng, lnb, ws, bsx, avg, tm)

        yb = _mix_b_call(z, zc, _natten_bias(b_rpb[l] * LOG2E, s // GRID_W))

        wconv = c_conv_w[l].reshape(3, 2 * N_PAIRS, LANES).transpose(1, 0, 2)
        bconv = c_conv_b[l].reshape(2 * N_PAIRS, 1, LANES)
        yc, ycc = _mix_c_call(z, zg, zc, zcg, wconv, bconv, cos_t, sin_t, esel, tri, dmask)

        weights = (w_out[l].astype(BF16), norm2_g[l][None, :], w_gate[l].astype(BF16), w_up[l].astype(BF16),
                   w_down[l].astype(BF16), final_g[None, :])
        x = _post_call(x, ya, yb, yc, mod, *weights, tm, last)
        if not last:
            yac = _mix_a_call(zc, lng, lnb, ws, bsx, avg, cl)
            ybc = _mix_b_ctx_call(zc)
            xc = _post_call(xc, yac, ybc, ycc, mod_c, *weights, cl, False)
    return x
```

```python
import functools

import numpy as np
import jax
import jax.numpy as jnp
from jax import lax
from jax.experimental import pallas as pl
from jax.experimental.pallas import tpu as pltpu

F32 = jnp.float32
BF16 = jnp.bfloat16

LANES = 128
MXU_N = 256
HEAD_DIM = 64
GRID_W = 64
A_GROUPS = 4
A_CHUNK = 128
B_HEADS = 6
C_HEADS = 6
WIN_ROWS = 8
WIN_COLS = 16
ROPE_BASE = 10000.0
EPS = 1e-6
N_MOD = 6
NEG = -1e30
LOG2E = float(np.log2(np.e))

A_TILES = 4
T_BQ, T_BK, T_BV, T_CQ, T_CK, T_CV, T_CO = 0, 3, 6, 9, 12, 15, 18
N_TILES = 21
N_PAIRS = 3
N_GATES = 4 * C_HEADS
Q_ROWS = 4
K_ROWS = Q_ROWS + WIN_ROWS
C_LEN = 256

VMEM_LIMIT = 56 * 1024 * 1024


def _cparams(sem):
    return pltpu.CompilerParams(dimension_semantics=sem, vmem_limit_bytes=VMEM_LIMIT)


def _dot(a, b):
    return jnp.dot(a, b, preferred_element_type=F32)


def _dot_nt(a, b):
    return lax.dot_general(a, b, (((1,), (1,)), ((), ())), preferred_element_type=F32)


def _split2(x):
    h = x.astype(BF16)
    return h, (x - h.astype(F32)).astype(BF16)


def _dot2_r(x, sel2):
    h, l = _split2(x)
    return _dot(jnp.concatenate([h, l], axis=1), sel2)


def _stack3(x):
    h = x.astype(BF16).astype(F32)
    r = x - h
    m = r.astype(BF16).astype(F32)
    return jnp.concatenate([h, m, r - m], axis=0).astype(BF16)


def _silu(x):
    h = 0.5 * x
    return h + h * jnp.tanh(h)


def _sigmoid(x):
    return 0.5 + 0.5 * jnp.tanh(0.5 * x)


def _gelu_tanh(x):
    return 0.5 * x * (1.0 + jnp.tanh(np.sqrt(2.0 / np.pi).astype(np.float32) * (x + 0.044715 * (x * x * x))))


def _log_sigmoid(x):
    return jnp.minimum(x, 0.0) - jnp.log1p(jnp.exp(-jnp.abs(x)))


def _rms_modulate(x, g, shift, scale):
    ms = jnp.mean(x * x, axis=-1, keepdims=True)
    return (x * lax.rsqrt(ms + EPS) * g) * (1.0 + scale) + shift


def _mod_kernel(c_ref, w_ref, b_ref, o_ref):
    o_ref[0] = _dot(_silu(c_ref[...]).astype(BF16), w_ref[0]) + b_ref[0, 0]


def _mod_call(cs, w_mod, b_mod):
    depth, d, nd = w_mod.shape
    rows = cs.shape[0]
    return pl.pallas_call(
        _mod_kernel,
        grid=(depth, nd // d),
        in_specs=[pl.BlockSpec((rows, d), lambda l, n: (0, 0)),
                  pl.BlockSpec((1, d, d), lambda l, n: (l, 0, n)),
                  pl.BlockSpec((1, 1, 1, d), lambda l, n: (l, n, 0, 0))],
        out_specs=pl.BlockSpec((1, rows, d), lambda l, n: (l, 0, n)),
        out_shape=jax.ShapeDtypeStruct((depth, rows, nd), F32),
        compiler_params=_cparams(("arbitrary", "arbitrary")),
        name="adaln_mod",
    )(cs, w_mod, b_mod.reshape(depth, nd // d, 1, d))


def _gmlp(u, v, j, lng_ref, lnb_ref, ws_ref, bsx_ref, avg):
    lane = lax.broadcasted_iota(jnp.int32, (A_CHUNK, LANES), 1)
    v = _gelu_tanh(v)
    dv = v - _dot2_r(v, avg)
    var = _dot2_r(dv * dv, avg)
    vn = (dv * lax.rsqrt(var + EPS) * lng_ref[j] + lnb_ref[j]).astype(BF16)
    u = _gelu_tanh(u)
    outs = []
    for c in range(u.shape[0] // A_CHUNK):
        rows = slice(c * A_CHUNK, (c + 1) * A_CHUNK)
        s = jnp.where(lane < HEAD_DIM, _dot(ws_ref[2 * j], vn[rows]), _dot(ws_ref[2 * j + 1], vn[rows])) + bsx_ref[j]
        outs.append(u[rows] * s)
    return jnp.concatenate(outs, axis=0)


def _inproj_kernel(x_ref, mod_ref, g_ref, w_ref, gbt_ref, lng_ref, lnb_ref, ws_ref, bsx_ref, avg_ref,
                   z_ref, zgt_ref, ya_ref):
    xm = _rms_modulate(x_ref[0], g_ref[...], mod_ref[0, 0:1, :], mod_ref[0, 1:2, :])
    xb = xm.astype(BF16)
    u = _dot(xb, w_ref[:, :MXU_N])
    v = _dot(xb, w_ref[:, MXU_N:2 * MXU_N])
    for j in range((N_TILES + 1) // 2):
        r = _dot(xb, w_ref[:, (A_TILES // 2 + j) * MXU_N:(A_TILES // 2 + j + 1) * MXU_N])
        z_ref[0, 2 * j] = r[:, :LANES]
        if 2 * j + 1 < N_TILES:
            z_ref[0, 2 * j + 1] = r[:, LANES:]
    g = r[:, LANES:].T[:N_GATES, :] + gbt_ref[...]
    sub = lax.broadcasted_iota(jnp.int32, g.shape, 0)
    zgt_ref[0] = jnp.where(sub % 8 >= 4, _log_sigmoid(g), g)
    for j in range(2):
        cols = slice(j * LANES, (j + 1) * LANES)
        ya_ref[0, j] = _gmlp(u[:, cols], v[:, cols], j, lng_ref, lnb_ref, ws_ref, bsx_ref, avg_ref[...])


def _inproj_call(x, mod, g, w, gbt, lng, lnb, ws, bsx, avg, tm):
    b, s, d = x.shape
    per_sample = mod.shape[0] == b
    mod_map = (lambda i, j: (i, 0, 0)) if per_sample else (lambda i, j: (0, 0, 0))
    const = lambda i, j: (0, 0)
    c3 = lambda i, j: (0, 0, 0)
    return pl.pallas_call(
        _inproj_kernel,
        grid=(b, s // tm),
        in_specs=[pl.BlockSpec((1, tm, d), lambda i, j: (i, j, 0)),
                  pl.BlockSpec((1, N_MOD, d), mod_map),
                  pl.BlockSpec((1, d), const),
                  pl.BlockSpec((d, (A_TILES + N_TILES + 1) * LANES), const),
                  pl.BlockSpec((N_GATES, tm), const),
                  pl.BlockSpec((2, 1, LANES), c3),
                  pl.BlockSpec((2, 1, LANES), c3),
                  pl.BlockSpec((A_GROUPS, A_CHUNK, A_CHUNK), c3),
                  pl.BlockSpec((2, A_CHUNK, LANES), c3),
                  pl.BlockSpec((2 * LANES, LANES), const)],
        out_specs=[pl.BlockSpec((1, N_TILES, tm, LANES), lambda i, j: (i, 0, j, 0)),
                   pl.BlockSpec((1, N_GATES, tm), lambda i, j: (i, 0, j)),
                   pl.BlockSpec((1, 2, tm, LANES), lambda i, j: (i, 0, j, 0))],
        out_shape=[jax.ShapeDtypeStruct((b, N_TILES, s, LANES), F32),
                   jax.ShapeDtypeStruct((b, N_GATES, s), F32),
                   jax.ShapeDtypeStruct((b, 2, s, LANES), F32)],
        compiler_params=_cparams(("arbitrary", "arbitrary")),
        name="in_proj",
    )(x, mod, g, w, gbt[:, :tm], lng, lnb, ws, bsx, avg)


def _mix_b_kernel(q_ref, k_ref, v_ref, kc_ref, vc_ref, bias_ref, o_ref, kb_ref, vb_ref, *, n_rows):
    n_blocks = n_rows // Q_ROWS
    nq = Q_ROWS * GRID_W
    nk = K_ROWS * GRID_W
    kb_ref[...] = k_ref[0, 0].astype(BF16)
    vb_ref[...] = v_ref[0, 0].astype(BF16)
    kc = kc_ref[0, 0].astype(BF16)
    vc = vc_ref[0, 0].astype(BF16)
    lane = lax.broadcasted_iota(jnp.int32, (nq, LANES), 1)

    def block_pair(i, carry):
        units = []
        for rb in (2 * i, 2 * i + 1):
            q_start = pl.multiple_of(rb * nq, nq)
            k_start = pl.multiple_of(jnp.clip(rb * Q_ROWS - WIN_ROWS // 2, 0, n_rows - K_ROWS) * GRID_W, nq)
            cls = jnp.where(rb == 0, 0, jnp.where(rb == n_blocks - 1, 2, 1))
            q = q_ref[0, 0, pl.ds(q_start, nq), :] * (HEAD_DIM ** -0.5 * LOG2E)
            kw = kb_ref[pl.ds(k_start, nk), :]
            for e in range(2):
                qe = jnp.where((lane < HEAD_DIM) == (e == 0), q, 0.0).astype(BF16)
                units.append((q_start, k_start, _dot_nt(qe, kw) + bias_ref[0, e, cls], _dot_nt(qe, kc)))
        probs = []
        for _, _, s_w, s_c in units:
            m = jnp.maximum(jnp.max(s_w, axis=-1, keepdims=True), jnp.max(s_c, axis=-1, keepdims=True))
            p_w = jnp.exp2(s_w - m)
            p_c = jnp.exp2(s_c - m)
            den = jnp.sum(p_w, axis=-1, keepdims=True) + jnp.sum(p_c, axis=-1, keepdims=True)
            probs.append((p_w.astype(BF16), p_c.astype(BF16), den))
        outs = [(_dot(p_w, vb_ref[pl.ds(k_start, nk), :]) + _dot(p_c, vc)) / den
                for (_, k_start, _, _), (p_w, p_c, den) in zip(units, probs)]
        for b2 in range(2):
            o_ref[0, 0, pl.ds(units[2 * b2][0], nq), :] = jnp.where(lane < HEAD_DIM, outs[2 * b2], outs[2 * b2 + 1])
        return carry

    lax.fori_loop(0, n_blocks // 2, block_pair, 0)


def _mix_b_call(z, zc, bias):
    b, _, s, _ = z.shape
    cl = zc.shape[2]
    n_rows = s // GRID_W
    assert n_rows % (2 * Q_ROWS) == 0 and n_rows >= K_ROWS
    zt = lambda t: pl.BlockSpec((1, 1, s, LANES), lambda p, i: (i, t + p, 0, 0))
    ct = lambda t: pl.BlockSpec((1, 1, cl, LANES), lambda p, i: (i, t + p, 0, 0))
    return pl.pallas_call(
        functools.partial(_mix_b_kernel, n_rows=n_rows),
        grid=(N_PAIRS, b),
        in_specs=[zt(T_BQ), zt(T_BK), zt(T_BV), ct(T_BK), ct(T_BV),
                  pl.BlockSpec((1, 2, 3, Q_ROWS * GRID_W, K_ROWS * GRID_W), lambda p, i: (p, 0, 0, 0, 0))],
        out_specs=pl.BlockSpec((1, 1, s, LANES), lambda p, i: (i, p, 0, 0)),
        out_shape=jax.ShapeDtypeStruct((b, N_PAIRS, s, LANES), F32),
        scratch_shapes=[pltpu.VMEM((s, LANES), BF16), pltpu.VMEM((s, LANES), BF16)],
        compiler_params=_cparams(("arbitrary", "arbitrary")),
        name="mix_b_natten",
    )(z, z, z, zc, zc, bias)


def _mix_b_ctx_kernel(q_ref, k_ref, v_ref, o_ref):
    q = q_ref[0, 0] * (HEAD_DIM ** -0.5)
    k = k_ref[0, 0].astype(BF16)
    v = v_ref[0, 0].astype(BF16)
    lane = lax.broadcasted_iota(jnp.int32, q.shape, 1)
    outs = []
    for e in range(2):
        qe = jnp.where((lane < HEAD_DIM) == (e == 0), q, 0.0).astype(BF16)
        s = _dot_nt(qe, k)
        p = jnp.exp(s - jnp.max(s, axis=-1, keepdims=True))
        outs.append(_dot(p.astype(BF16), v) / jnp.sum(p, axis=-1, keepdims=True))
    o_ref[0, 0] = jnp.where(lane < HEAD_DIM, outs[0], outs[1])


def _mix_b_ctx_call(zc):
    b, _, cl, _ = zc.shape
    blk = lambda t: pl.BlockSpec((1, 1, cl, LANES), lambda i, p: (i, t + p, 0, 0))
    return pl.pallas_call(
        _mix_b_ctx_kernel,
        grid=(b, N_PAIRS),
        in_specs=[blk(T_BQ), blk(T_BK), blk(T_BV)],
        out_specs=blk(0),
        out_shape=jax.ShapeDtypeStruct((b, N_PAIRS, cl, LANES), F32),
        compiler_params=_cparams(("arbitrary", "arbitrary")),
        name="mix_b_ctx",
    )(zc, zc, zc)


def _natten_bias(rpb, n_rows):
    h = rpb.shape[0]
    padded = jnp.pad(rpb, ((0, 0), (0, 0), (GRID_W, GRID_W)))
    toe = jnp.stack([padded[:, :, GRID_W + WIN_COLS - 1 - c:2 * GRID_W + WIN_COLS - 1 - c] for c in range(GRID_W)], axis=2)
    c = np.arange(GRID_W)[:, None]
    kc = np.arange(GRID_W)[None, :]
    cs = np.clip(c - WIN_COLS // 2, 0, GRID_W - WIN_COLS)
    col_ok = (kc >= cs) & (kc < cs + WIN_COLS)
    toe = jnp.where(col_ok[None, None], toe, NEG)
    neg_block = jnp.full((h, GRID_W, GRID_W), NEG, F32)
    classes = []
    for r0 in (0, Q_ROWS, n_rows - Q_ROWS):
        k0 = int(np.clip(r0 - WIN_ROWS // 2, 0, n_rows - K_ROWS))
        q_rows = []
        for i in range(Q_ROWS):
            r = r0 + i
            rs = int(np.clip(r - WIN_ROWS // 2, 0, n_rows - WIN_ROWS))
            blocks = []
            for j in range(K_ROWS):
                kr = k0 + j
                blocks.append(toe[:, kr - r + WIN_ROWS - 1] if rs <= kr < rs + WIN_ROWS else neg_block)
            q_rows.append(jnp.concatenate(blocks, axis=-1))
        classes.append(jnp.concatenate(q_rows, axis=1))
    bias = jnp.stack(classes, axis=1)
    return bias.reshape(N_PAIRS, 2, 3, Q_ROWS * GRID_W, K_ROWS * GRID_W)


def _mlstm_chunks(items, dmask_ref, head0, ones_blk, bd_mask):
    n_l = items[0][1].shape[0]
    half = n_l // 2
    zeros = jnp.zeros((half, half), BF16)

    early = []
    for (direction, qb, q2, kt, v, v_heads, fc, w, b_rows, state) in items:
        st, m_pl = state
        s_heads = _dot(q2, kt)
        inter = _dot(qb, st.astype(BF16))
        w_max = jnp.max(w, axis=0, keepdims=True)
        a_s = jnp.exp2(w - w_max)
        upd = jnp.where(bd_mask, _dot(kt, jnp.concatenate([a_s * v, a_s], axis=1).astype(BF16)), 0.0)
        f_tot = fc[n_l - 1:n_l, :] if direction == 0 else fc[0:1, :]
        mw = jnp.maximum(m_pl, w_max)
        a_prev = jnp.exp2(m_pl - mw)
        a_new = jnp.exp2(w_max - mw)
        st_new = jnp.concatenate([a_prev, a_prev], axis=1) * st + jnp.concatenate([a_new, a_new], axis=1) * upd
        early.append((s_heads, inter, m_pl, st_new, f_tot + mw))

    weights = []
    for (direction, qb, q2, kt, v, v_heads, fc, w, b_rows, state), (s_heads, _, _, _, _) in zip(items, early):
        first = slice(0, half) if direction == 0 else slice(half, n_l)
        last = slice(half, n_l) if direction == 0 else slice(0, half)
        p_heads, cm_heads = [], []
        for e in range(2):
            s = s_heads[e * n_l:(e + 1) * n_l]
            bm_a = b_rows[e][:, first] + dmask_ref[direction, first, first]
            bm_b = b_rows[e] + dmask_ref[direction, last, :]
            cm_a = jnp.max(bm_a, axis=-1, keepdims=True)
            cm_b = jnp.max(bm_b, axis=-1, keepdims=True)
            p_a = (s[first, first] * jnp.exp2(bm_a - cm_a)).astype(BF16)
            p_b = (s[last, :] * jnp.exp2(bm_b - cm_b)).astype(BF16)
            if direction == 0:
                p_heads.append(jnp.concatenate([jnp.concatenate([p_a, zeros], axis=1), p_b], axis=0))
                cm_heads.append(jnp.concatenate([cm_a, cm_b], axis=0))
            else:
                p_heads.append(jnp.concatenate([p_b, jnp.concatenate([zeros, p_a], axis=1)], axis=0))
                cm_heads.append(jnp.concatenate([cm_b, cm_a], axis=0))
        weights.append((jnp.concatenate(p_heads, axis=1), jnp.where(head0, cm_heads[0], cm_heads[1])))

    intras = [_dot(p, jnp.concatenate([item[5], ones_blk], axis=1)) for item, (p, _) in zip(items, weights)]

    results = []
    for item, (_, inter, m_pl, st_new, m_new), (_, cm_pl), intra in zip(items, early, weights, intras):
        fc = item[6]
        mx = jnp.maximum(m_pl, cm_pl)
        w_intra = jnp.exp2(cm_pl - mx)
        w_inter = jnp.exp2(m_pl - mx)
        num = w_inter * inter[:, :LANES] + w_intra * intra[:, :LANES]
        den = w_inter * inter[:, LANES:] + w_intra * intra[:, LANES:]
        results.append((num / jnp.maximum(jnp.abs(den), jnp.exp2(-(fc + mx))), st_new, m_new))
    return results


def _mix_c_kernel(zq_ref, zk_ref, zv_ref, zo_ref, rg_ref, cq_ref, ck_ref, cv_ref, co_ref, crg_ref,
                  wq_ref, wk_ref, bq_ref, bk_ref, cos_ref, sin_ref, esel_ref, tri_ref, dmask_ref,
                  o_ref, oc_ref, qs_ref, q2_ref, kt_ref, vh_ref, fw_ref, br_ref):
    s_len = zq_ref.shape[2]
    n_l = C_LEN
    n_chunks = s_len // n_l
    lane = lax.broadcasted_iota(jnp.int32, (n_l, LANES), 1)
    row = lax.broadcasted_iota(jnp.int32, (n_l, LANES), 0)
    k_scale = HEAD_DIM ** -0.5

    def conv_silu(x_ref, t0, total, w_ref_, b_ref_):
        x = x_ref[0, 0, pl.ds(t0, n_l), :]
        if total == n_l:
            prev_row = next_row = jnp.zeros((1, LANES), F32)
        else:
            prev_row = x_ref[0, 0, pl.ds(jnp.maximum(t0 - 1, 0), 1), :] * (t0 > 0).astype(F32)
            next_row = x_ref[0, 0, pl.ds(jnp.minimum(t0 + n_l, total - 1), 1), :] * (t0 + n_l < total).astype(F32)
        x_prev = jnp.where(row == 0, prev_row, pltpu.roll(x, 1, axis=0))
        x_next = jnp.where(row == n_l - 1, next_row, pltpu.roll(x, n_l - 1, axis=0))
        y = x_prev * w_ref_[0, 0:1, :] + x * w_ref_[0, 1:2, :] + x_next * w_ref_[0, 2:3, :] + b_ref_[0]
        return _silu(y)

    def rope(x, t0):
        cos = cos_ref[pl.ds(t0, n_l), :]
        sin = sin_ref[pl.ds(t0, n_l), :]
        first = (lane % (HEAD_DIM // 2)) < (HEAD_DIM // 4)
        partner = jnp.where(first, pltpu.roll(x, LANES - HEAD_DIM // 4, axis=1), pltpu.roll(x, HEAD_DIM // 4, axis=1))
        return x * cos + partner * sin

    def qk_prep(q_in, k_in, t0, total, with_rope):
        q = conv_silu(q_in, t0, total, wq_ref, bq_ref)
        k = conv_silu(k_in, t0, total, wk_ref, bk_ref)
        if with_rope:
            q, k = rope(q, t0), rope(k, t0)
        return q, (k * k_scale).T.astype(BF16)

    def gate_prep(rg, d):
        n_g = rg.shape[0]
        f3 = _dot(_stack3(rg), tri_ref[1 - d])
        f_row = f3[:n_g] + f3[n_g:2 * n_g] + f3[2 * n_g:]
        b_rows = [rg[2 * d + e:2 * d + e + 1, :] - f_row[4 + 2 * d + e:5 + 2 * d + e, :] for e in range(2)]
        fw = lax.dot_general(_stack3(jnp.concatenate([f_row, rg], axis=0)), esel_ref[d], (((0,), (0,)), ((), ())),
                             preferred_element_type=F32)
        return fw[:, :LANES], fw[:, LANES:], b_rows

    head0 = lane < HEAD_DIM

    def per_head(x):
        return jnp.concatenate([jnp.where(head0, x, 0.0), jnp.where(head0, 0.0, x)], axis=0).astype(BF16)

    ones_blk = jnp.concatenate([jnp.where(head0, 1.0, 0.0), jnp.where(head0, 0.0, 1.0)], axis=0).astype(BF16)
    bd_r = lax.broadcasted_iota(jnp.int32, (LANES, 2 * LANES), 0)
    bd_c = lax.broadcasted_iota(jnp.int32, (LANES, 2 * LANES), 1)
    bd_mask = (bd_r < HEAD_DIM) == ((bd_c % LANES) < HEAD_DIM)
    consts = (head0, ones_blk, bd_mask)

    cq, ckt = qk_prep(cq_ref, ck_ref, 0, n_l, False)
    cv = cv_ref[0, 0]
    crg = crg_ref[0, 0] * LOG2E
    items = []
    for d in range(2):
        items.append((d, cq.astype(BF16), per_head(cq), ckt, cv, per_head(cv), *gate_prep(crg, d),
                      (jnp.zeros((LANES, 2 * LANES), F32), jnp.zeros((1, LANES), F32))))
    res = _mlstm_chunks(items, dmask_ref, *consts)
    oc_ref[0, 0] = (res[0][0] + res[1][0]) * _sigmoid(co_ref[0, 0])
    carry = [(st, m_pl) for _, st, m_pl in res]

    def prep(c, carry):
        t0 = pl.multiple_of(c * n_l, n_l)
        rows = pl.ds(t0, n_l)
        q, kt_ref[:, rows] = qk_prep(zq_ref, zk_ref, t0, s_len, True)
        qs_ref[rows, :] = q.astype(BF16)
        q2_ref[:, rows, :] = per_head(q).reshape(2, n_l, LANES)
        vh_ref[:, rows, :] = per_head(zv_ref[0, 0, rows, :]).reshape(2, n_l, LANES)
        rg = rg_ref[0, 0, :, rows] * LOG2E
        for d in range(2):
            fc, w, b_rows = gate_prep(rg, d)
            fw_ref[d, rows, :] = jnp.concatenate([fc, w], axis=1)
            for e in range(2):
                br_ref[2 * d + e:2 * d + e + 1, rows] = b_rows[e]
        return carry

    lax.fori_loop(0, n_chunks, prep, 0, unroll=4)

    def make_step(second_visit):
        def step(i, carry):
            items, all_rows = [], []
            for d in range(2):
                c = i if d == 0 else n_chunks - 1 - i
                rows = pl.ds(pl.multiple_of(c * n_l, n_l), n_l)
                fw = fw_ref[d, rows, :]
                br = br_ref[:, rows]
                b_rows = [br[2 * d + e:2 * d + e + 1, :] for e in range(2)]
                items.append((d, qs_ref[rows, :], q2_ref[:, rows, :].reshape(2 * n_l, LANES), kt_ref[:, rows],
                              zv_ref[0, 0, rows, :], vh_ref[:, rows, :].reshape(2 * n_l, LANES),
                              fw[:, :LANES], fw[:, LANES:], b_rows, carry[d]))
                all_rows.append(rows)
            res = _mlstm_chunks(items, dmask_ref, *consts)
            for rows, (h, _, _) in zip(all_rows, res):
                if second_visit:
                    o_ref[0, 0, rows, :] = (o_ref[0, 0, rows, :] + h) * _sigmoid(zo_ref[0, 0, rows, :])
                else:
                    o_ref[0, 0, rows, :] = h
            return tuple((st, m_pl) for _, st, m_pl in res)
        return step

    carry = lax.fori_loop(0, n_chunks // 2, make_step(False), tuple(carry), unroll=2)
    lax.fori_loop(n_chunks // 2, n_chunks, make_step(True), carry, unroll=2)


def _mix_c_call(z, zg, zc, zcg, wconv, bconv, cos_t, sin_t, esel, tri, dmask):
    b, _, s, _ = z.shape
    cl = zc.shape[2]
    assert cl == C_LEN and s % (4 * C_LEN) == 0
    once = dict(pipeline_mode=pl.Buffered(1))
    zt = lambda t: pl.BlockSpec((1, 1, s, LANES), lambda i, p: (i, t + p, 0, 0))
    ct = lambda t: pl.BlockSpec((1, 1, cl, LANES), lambda i, p: (i, t + p, 0, 0))
    return pl.pallas_call(
        _mix_c_kernel,
        grid=(b, N_PAIRS),
        in_specs=[zt(T_CQ), zt(T_CK), zt(T_CV), zt(T_CO),
                  pl.BlockSpec((1, 1, 8, s), lambda i, p: (i, p, 0, 0)),
                  ct(T_CQ), ct(T_CK), ct(T_CV), ct(T_CO),
                  pl.BlockSpec((1, 1, 8, cl), lambda i, p: (i, p, 0, 0)),
                  pl.BlockSpec((1, 3, LANES), lambda i, p: (p, 0, 0)),
                  pl.BlockSpec((1, 3, LANES), lambda i, p: (N_PAIRS + p, 0, 0)),
                  pl.BlockSpec((1, 1, LANES), lambda i, p: (p, 0, 0)),
                  pl.BlockSpec((1, 1, LANES), lambda i, p: (N_PAIRS + p, 0, 0)),
                  pl.BlockSpec((s, LANES), lambda i, p: (0, 0), **once),
                  pl.BlockSpec((s, LANES), lambda i, p: (0, 0), **once),
                  pl.BlockSpec((2, 48, 2 * LANES), lambda i, p: (0, 0, 0), **once),
                  pl.BlockSpec((2, C_LEN, C_LEN), lambda i, p: (0, 0, 0), **once),
                  pl.BlockSpec((2, C_LEN, C_LEN), lambda i, p: (0, 0, 0), **once)],
        out_specs=[pl.BlockSpec((1, 1, s, LANES), lambda i, p: (i, p, 0, 0)),
                   pl.BlockSpec((1, 1, cl, LANES), lambda i, p: (i, p, 0, 0))],
        out_shape=[jax.ShapeDtypeStruct((b, N_PAIRS, s, LANES), F32),
                   jax.ShapeDtypeStruct((b, N_PAIRS, cl, LANES), F32)],
        scratch_shapes=[pltpu.VMEM((s, LANES), BF16), pltpu.VMEM((2, s, LANES), BF16), pltpu.VMEM((LANES, s), BF16),
                        pltpu.VMEM((2, s, LANES), BF16), pltpu.VMEM((2, s, 2 * LANES), F32), pltpu.VMEM((8, s), F32)],
        compiler_params=_cparams(("arbitrary", "arbitrary")),
        name="mix_c_mlstm",
    )(z, z, z, z, zg.reshape(b, N_PAIRS, 8, s), zc, zc, zc, zc, zcg.reshape(b, N_PAIRS, 8, cl),
      wconv, wconv, bconv, bconv, cos_t, sin_t, esel, tri, dmask)


def _rope_tables(s):
    t = jnp.arange(s)
    quarter = HEAD_DIM // 4
    inv_freq = ROPE_BASE ** (-jnp.arange(quarter, dtype=F32) / quarter)
    lane = np.arange(LANES) % HEAD_DIM
    use_col = jnp.asarray(lane >= HEAD_DIM // 2)
    second = jnp.asarray((lane % (HEAD_DIM // 2)) >= quarter)
    freq = inv_freq[np.asarray(lane % quarter)]
    pos = jnp.where(use_col[None, :], (t % GRID_W)[:, None], (t // GRID_W)[:, None]).astype(F32)
    ang = pos * freq[None, :]
    return jnp.cos(ang), jnp.where(second[None, :], jnp.sin(ang), -jnp.sin(ang))


def _gate_tables():
    esel = np.zeros((2, 48, 2 * LANES), np.float32)
    for d in range(2):
        for ln in range(LANES):
            e = ln // HEAD_DIM
            esel[d, 4 + 2 * d + e, ln] = 1.0
            esel[d, 4 + 2 * d + e, LANES + ln] = -1.0
            esel[d, 8 + 2 * d + e, LANES + ln] = 1.0
    esel[:, 16:32, :] = esel[:, :16, :]
    esel[:, 32:, :] = esel[:, :16, :]
    t = np.arange(C_LEN)
    tri = np.stack([(t[None, :] <= t[:, None]), (t[None, :] >= t[:, None])])
    dmask = np.where(tri, 0.0, NEG).astype(np.float32)
    return jnp.asarray(esel, BF16), jnp.asarray(tri.astype(np.float32), BF16), jnp.asarray(dmask)


def _post_kernel(x_ref, ya_ref, yb_ref, yc_ref, mod_ref, wout_ref, g_ref, wg_ref, wu_ref, wd_ref, fg_ref, o_ref,
                 *, final, ff_chunk):
    y = jnp.concatenate([ya_ref[0, 0], ya_ref[0, 1], yb_ref[0, 0], yb_ref[0, 1], yb_ref[0, 2],
                         yc_ref[0, 0], yc_ref[0, 1], yc_ref[0, 2]], axis=1).astype(BF16)
    x1 = x_ref[0] + mod_ref[0, 2:3, :] * _dot(y, wout_ref[...])
    hm = _rms_modulate(x1, g_ref[...], mod_ref[0, 3:4, :], mod_ref[0, 4:5, :]).astype(BF16)
    acc = jnp.zeros_like(x1)
    for c in range(wg_ref.shape[1] // ff_chunk):
        cols = slice(c * ff_chunk, (c + 1) * ff_chunk)
        a = _silu(_dot(hm, wg_ref[:, cols])) * _dot(hm, wu_ref[:, cols])
        acc = acc + _dot(a.astype(BF16), wd_ref[cols, :])
    x2 = x1 + mod_ref[0, 5:6, :] * acc
    if final:
        ms = jnp.mean(x2 * x2, axis=-1, keepdims=True)
        x2 = x2 * lax.rsqrt(ms + EPS) * fg_ref[...]
    o_ref[0] = x2


def _post_call(x, ya, yb, yc, mod, wout, g, wg, wu, wd, fg, tm, final):
    b, s, d = x.shape
    dff = wg.shape[1]
    per_sample = mod.shape[0] == b
    mod_map = (lambda i, j: (i, 0, 0)) if per_sample else (lambda i, j: (0, 0, 0))
    const = lambda i, j: (0, 0)
    once = dict(pipeline_mode=pl.Buffered(1))
    yt = lambda n: pl.BlockSpec((1, n, tm, LANES), lambda i, j: (i, 0, j, 0))
    return pl.pallas_call(
        functools.partial(_post_kernel, final=final, ff_chunk=MXU_N),
        grid=(b, s // tm),
        in_specs=[pl.BlockSpec((1, tm, d), lambda i, j: (i, j, 0)),
                  yt(2), yt(N_PAIRS), yt(N_PAIRS),
                  pl.BlockSpec((1, N_MOD, d), mod_map),
                  pl.BlockSpec((d, d), const, **once),
                  pl.BlockSpec((1, d), const),
                  pl.BlockSpec((d, dff), const, **once),
                  pl.BlockSpec((d, dff), const, **once),
                  pl.BlockSpec((dff, d), const, **once),
                  pl.BlockSpec((1, d), const)],
        out_specs=pl.BlockSpec((1, tm, d), lambda i, j: (i, j, 0)),
        out_shape=jax.ShapeDtypeStruct((b, s, d), F32),
        compiler_params=_cparams(("arbitrary", "arbitrary")),
        name="out_proj_swiglu",
    )(x, ya, yb, yc, mod, wout, g, wg, wu, wd, fg)


def _gate_perm():
    row = np.zeros(N_GATES, np.int32)
    for p in range(N_PAIRS):
        for k in range(4):
            for e in range(2):
                row[8 * p + 4 * (k % 2) + 2 * (k // 2) + e] = C_HEADS * k + 2 * p + e
    return row


def kernel(x, c, ctx, c_ctx, w_mod, b_mod, norm1_g, w_in, a_ln_g, a_ln_b, a_ws, a_bs, b_rpb,
           c_conv_w, c_conv_b, c_gate_b, w_out, norm2_g, w_gate, w_up, w_down, final_g):
    b, s, d = x.shape
    cl = ctx.shape[1]
    depth = w_mod.shape[0]
    n_main = (A_TILES + N_TILES) * LANES
    tm = 512
    perm = _gate_perm()

    pad = (-(b + 1)) % 8
    cs = jnp.concatenate([c, c_ctx[None, :], jnp.zeros((pad, d), F32)], axis=0)
    mod_all = _mod_call(cs, w_mod.astype(BF16), b_mod)

    cos_t, sin_t = _rope_tables(s)
    esel, tri, dmask = _gate_tables()
    avg = np.kron(np.eye(2), np.full((HEAD_DIM, HEAD_DIM), 1.0 / HEAD_DIM))
    avg = jnp.asarray(np.concatenate([avg, avg], axis=0), BF16)

    xc = ctx
    for l in range(depth):
        last = l == depth - 1
        mod = mod_all[l, :b].reshape(b, N_MOD, d)
        mod_c = mod_all[l, b:b + 1].reshape(1, N_MOD, d)

        w_main = jnp.concatenate([w_in[l][:, :n_main], w_in[l][:, n_main:][:, perm],
                                  jnp.zeros((d, LANES - N_GATES), F32)], axis=1).astype(BF16)
        gbt = jnp.broadcast_to(c_gate_b[l].reshape(-1)[perm][:, None], (N_GATES, tm))
        g1 = norm1_g[l][None, :]

        mix_a = (a_ln_g[l].reshape(2, 1, LANES), a_ln_b[l].reshape(2, 1, LANES), a_ws[l].astype(BF16),
                 jnp.repeat(a_bs[l].reshape(2, 2, A_CHUNK).transpose(0, 2, 1), HEAD_DIM, axis=2), avg)
        z, zg, ya = _inproj_call(x, mod, g1, w_main, gbt, *mix_a, tm)
        zc, zcg, yac = _inproj_call(xc, mod_c, g1, w_main, gbt, *mix_a, cl)

        yb = _mix_b_call(z, zc, _natten_bias(b_rpb[l] * LOG2E, s // GRID_W))

        wconv = c_conv_w[l].reshape(3, 2 * N_PAIRS, LANES).transpose(1, 0, 2)
        bconv = c_conv_b[l].reshape(2 * N_PAIRS, 1, LANES)
        yc, ycc = _mix_c_call(z, zg, zc, zcg, wconv, bconv, cos_t, sin_t, esel, tri, dmask)

        weights = (w_out[l].astype(BF16), norm2_g[l][None, :], w_gate[l].astype(BF16), w_up[l].astype(BF16),
                   w_down[l].astype(BF16), final_g[None, :])
        x = _post_call(x, ya, yb, yc, mod, *weights, tm, last)
        if not last:
            ybc = _mix_b_ctx_call(zc)
            xc = _post_call(xc, yac, ybc, ycc, mod_c, *weights, cl, False)
    return x
```

```python
import functools

import numpy as np
import jax
import jax.numpy as jnp
from jax import lax
from jax.experimental import pallas as pl
from jax.experimental.pallas import tpu as pltpu

F32 = jnp.float32
BF16 = jnp.bfloat16

LANES = 128
MXU_N = 256
HEAD_DIM = 64
GRID_W = 64
A_GROUPS = 4
A_CHUNK = 128
B_HEADS = 6
C_HEADS = 6
WIN_ROWS = 8
WIN_COLS = 16
ROPE_BASE = 10000.0
EPS = 1e-6
N_MOD = 6
NEG = -1e30
LOG2E = float(np.log2(np.e))

A_TILES = 4
T_BQ, T_BK, T_BV, T_CQ, T_CK, T_CV, T_CO = 0, 3, 6, 9, 12, 15, 18
N_TILES = 21
N_PAIRS = 3
N_GATES = 4 * C_HEADS
PAIR_GATES = N_GATES // N_PAIRS
Q_ROWS = 4
K_ROWS = Q_ROWS + WIN_ROWS
C_LEN = 256

VMEM_LIMIT = 56 * 1024 * 1024


def _cparams(sem):
    return pltpu.CompilerParams(dimension_semantics=sem, vmem_limit_bytes=VMEM_LIMIT)


def _dot(a, b):
    return jnp.dot(a, b, preferred_element_type=F32)


def _dot_nt(a, b):
    return lax.dot_general(a, b, (((1,), (1,)), ((), ())), preferred_element_type=F32)


def _split2(x):
    h = x.astype(BF16)
    return h, (x - h.astype(F32)).astype(BF16)


def _dot2_r(x, sel2):
    h, l = _split2(x)
    return _dot(jnp.concatenate([h, l], axis=1), sel2)


def _stack3(x):
    h = x.astype(BF16).astype(F32)
    r = x - h
    m = r.astype(BF16).astype(F32)
    return jnp.concatenate([h, m, r - m], axis=0).astype(BF16)


def _silu(x):
    h = 0.5 * x
    return h + h * jnp.tanh(h)


def _sigmoid(x):
    return 0.5 + 0.5 * jnp.tanh(0.5 * x)


def _gelu_tanh(x):
    return 0.5 * x * (1.0 + jnp.tanh(np.sqrt(2.0 / np.pi).astype(np.float32) * (x + 0.044715 * (x * x * x))))


def _log_sigmoid(x):
    return jnp.minimum(x, 0.0) - jnp.log1p(jnp.exp(-jnp.abs(x)))


def _rms_modulate(x, g, shift, scale):
    ms = jnp.mean(x * x, axis=-1, keepdims=True)
    return (x * lax.rsqrt(ms + EPS) * g) * (1.0 + scale) + shift


def _mod_kernel(c_ref, w_ref, b_ref, o_ref):
    o_ref[0] = _dot(_silu(c_ref[...]).astype(BF16), w_ref[0]) + b_ref[0, 0]


def _mod_call(cs, w_mod, b_mod):
    depth, d, nd = w_mod.shape
    rows = cs.shape[0]
    return pl.pallas_call(
        _mod_kernel,
        grid=(depth, nd // d),
        in_specs=[pl.BlockSpec((rows, d), lambda l, n: (0, 0)),
                  pl.BlockSpec((1, d, d), lambda l, n: (l, 0, n)),
                  pl.BlockSpec((1, 1, 1, d), lambda l, n: (l, n, 0, 0))],
        out_specs=pl.BlockSpec((1, rows, d), lambda l, n: (l, 0, n)),
        out_shape=jax.ShapeDtypeStruct((depth, rows, nd), F32),
        compiler_params=_cparams(("arbitrary", "arbitrary")),
        name="adaln_mod",
    )(cs, w_mod, b_mod.reshape(depth, nd // d, 1, d))


def _gmlp(u, v, j, lng_ref, lnb_ref, ws_ref, bsx_ref, avg):
    lane = lax.broadcasted_iota(jnp.int32, (A_CHUNK, LANES), 1)
    v = _gelu_tanh(v)
    dv = v - _dot2_r(v, avg)
    var = _dot2_r(dv * dv, avg)
    vn = (dv * lax.rsqrt(var + EPS) * lng_ref[j] + lnb_ref[j]).astype(BF16)
    u = _gelu_tanh(u)
    outs = []
    for c in range(u.shape[0] // A_CHUNK):
        rows = slice(c * A_CHUNK, (c + 1) * A_CHUNK)
        s = jnp.where(lane < HEAD_DIM, _dot(ws_ref[2 * j], vn[rows]), _dot(ws_ref[2 * j + 1], vn[rows])) + bsx_ref[j]
        outs.append(u[rows] * s)
    return jnp.concatenate(outs, axis=0)


def _inproj_kernel(x_ref, mod_ref, g_ref, w_ref, gbt_ref, lng_ref, lnb_ref, ws_ref, bsx_ref, avg_ref,
                   z_ref, zgt_ref, ya_ref):
    xm = _rms_modulate(x_ref[0], g_ref[...], mod_ref[0, 0:1, :], mod_ref[0, 1:2, :])
    xb = xm.astype(BF16)
    u = _dot(xb, w_ref[:, :MXU_N])
    v = _dot(xb, w_ref[:, MXU_N:2 * MXU_N])
    for j in range((N_TILES + 1) // 2):
        r = _dot(xb, w_ref[:, (A_TILES // 2 + j) * MXU_N:(A_TILES // 2 + j + 1) * MXU_N])
        z_ref[0, 2 * j] = r[:, :LANES]
        if 2 * j + 1 < N_TILES:
            z_ref[0, 2 * j + 1] = r[:, LANES:]
    g = r[:, LANES:].T[:N_GATES, :] + gbt_ref[...]
    sub = lax.broadcasted_iota(jnp.int32, g.shape, 0)
    zgt_ref[0] = jnp.where(sub % PAIR_GATES >= PAIR_GATES // 2, _log_sigmoid(g), g)
    for j in range(2):
        cols = slice(j * LANES, (j + 1) * LANES)
        ya_ref[0, j] = _gmlp(u[:, cols], v[:, cols], j, lng_ref, lnb_ref, ws_ref, bsx_ref, avg_ref[...])


def _inproj_call(x, mod, g, w, gbt, lng, lnb, ws, bsx, avg, tm):
    b, s, d = x.shape
    per_sample = mod.shape[0] == b
    mod_map = (lambda i, j: (i, 0, 0)) if per_sample else (lambda i, j: (0, 0, 0))
    const = lambda i, j: (0, 0)
    c3 = lambda i, j: (0, 0, 0)
    return pl.pallas_call(
        _inproj_kernel,
        grid=(b, s // tm),
        in_specs=[pl.BlockSpec((1, tm, d), lambda i, j: (i, j, 0)),
                  pl.BlockSpec((1, N_MOD, d), mod_map),
                  pl.BlockSpec((1, d), const),
                  pl.BlockSpec((d, (A_TILES + N_TILES + 1) * LANES), const),
                  pl.BlockSpec((N_GATES, tm), const),
                  pl.BlockSpec((2, 1, LANES), c3),
                  pl.BlockSpec((2, 1, LANES), c3),
                  pl.BlockSpec((A_GROUPS, A_CHUNK, A_CHUNK), c3),
                  pl.BlockSpec((2, A_CHUNK, LANES), c3),
                  pl.BlockSpec((2 * LANES, LANES), const)],
        out_specs=[pl.BlockSpec((1, N_TILES, tm, LANES), lambda i, j: (i, 0, j, 0)),
                   pl.BlockSpec((1, N_GATES, tm), lambda i, j: (i, 0, j)),
                   pl.BlockSpec((1, 2, tm, LANES), lambda i, j: (i, 0, j, 0))],
        out_shape=[jax.ShapeDtypeStruct((b, N_TILES, s, LANES), F32),
                   jax.ShapeDtypeStruct((b, N_GATES, s), F32),
                   jax.ShapeDtypeStruct((b, 2, s, LANES), F32)],
        compiler_params=_cparams(("arbitrary", "arbitrary")),
        name="in_proj",
    )(x, mod, g, w, gbt[:, :tm], lng, lnb, ws, bsx, avg)


def _mix_b_kernel(q_ref, k_ref, v_ref, kc_ref, vc_ref, bias_ref, o_ref, kb_ref, vb_ref, *, n_rows):
    n_blocks = n_rows // Q_ROWS
    nq = Q_ROWS * GRID_W
    nk = K_ROWS * GRID_W
    kb_ref[...] = k_ref[0, 0].astype(BF16)
    vb_ref[...] = v_ref[0, 0].astype(BF16)
    kc = kc_ref[0, 0].astype(BF16)
    vc = vc_ref[0, 0].astype(BF16)
    lane = lax.broadcasted_iota(jnp.int32, (nq, LANES), 1)

    def block_pair(i, carry):
        units = []
        for rb in (2 * i, 2 * i + 1):
            q_start = pl.multiple_of(rb * nq, nq)
            k_start = pl.multiple_of(jnp.clip(rb * Q_ROWS - WIN_ROWS // 2, 0, n_rows - K_ROWS) * GRID_W, nq)
            cls = jnp.where(rb == 0, 0, jnp.where(rb == n_blocks - 1, 2, 1))
            q = q_ref[0, 0, pl.ds(q_start, nq), :] * (HEAD_DIM ** -0.5 * LOG2E)
            kw = kb_ref[pl.ds(k_start, nk), :]
            for e in range(2):
                qe = jnp.where((lane < HEAD_DIM) == (e == 0), q, 0.0).astype(BF16)
                units.append((q_start, k_start, _dot_nt(qe, kw) + bias_ref[0, e, cls], _dot_nt(qe, kc)))
        probs = []
        for _, _, s_w, s_c in units:
            m = jnp.maximum(jnp.max(s_w, axis=-1, keepdims=True), jnp.max(s_c, axis=-1, keepdims=True))
            p_w = jnp.exp2(s_w - m)
            p_c = jnp.exp2(s_c - m)
            den = jnp.sum(p_w, axis=-1, keepdims=True) + jnp.sum(p_c, axis=-1, keepdims=True)
            probs.append((p_w.astype(BF16), p_c.astype(BF16), den))
        outs = [(_dot(p_w, vb_ref[pl.ds(k_start, nk), :]) + _dot(p_c, vc)) / den
                for (_, k_start, _, _), (p_w, p_c, den) in zip(units, probs)]
        for b2 in range(2):
            o_ref[0, 0, pl.ds(units[2 * b2][0], nq), :] = jnp.where(lane < HEAD_DIM, outs[2 * b2], outs[2 * b2 + 1])
        return carry

    lax.fori_loop(0, n_blocks // 2, block_pair, 0)


def _mix_b_call(z, zc, bias, layer):
    b, _, s, _ = z.shape
    cl = zc.shape[2]
    n_rows = s // GRID_W
    assert n_rows % (2 * Q_ROWS) == 0 and n_rows >= K_ROWS
    zt = lambda t: pl.BlockSpec((1, 1, s, LANES), lambda p, i: (i, t + p, 0, 0))
    ct = lambda t: pl.BlockSpec((1, 1, cl, LANES), lambda p, i: (i, t + p, 0, 0))
    return pl.pallas_call(
        functools.partial(_mix_b_kernel, n_rows=n_rows),
        grid=(N_PAIRS, b),
        in_specs=[zt(T_BQ), zt(T_BK), zt(T_BV), ct(T_BK), ct(T_BV),
                  pl.BlockSpec((1, 2, 3, Q_ROWS * GRID_W, K_ROWS * GRID_W), lambda p, i: (layer * N_PAIRS + p, 0, 0, 0, 0))],
        out_specs=pl.BlockSpec((1, 1, s, LANES), lambda p, i: (i, p, 0, 0)),
        out_shape=jax.ShapeDtypeStruct((b, N_PAIRS, s, LANES), F32),
        scratch_shapes=[pltpu.VMEM((s, LANES), BF16), pltpu.VMEM((s, LANES), BF16)],
        compiler_params=_cparams(("arbitrary", "arbitrary")),
        name="mix_b_natten",
    )(z, z, z, zc, zc, bias)


def _mix_b_ctx_kernel(q_ref, k_ref, v_ref, o_ref):
    q = q_ref[0, 0] * (HEAD_DIM ** -0.5)
    k = k_ref[0, 0].astype(BF16)
    v = v_ref[0, 0].astype(BF16)
    lane = lax.broadcasted_iota(jnp.int32, q.shape, 1)
    outs = []
    for e in range(2):
        qe = jnp.where((lane < HEAD_DIM) == (e == 0), q, 0.0).astype(BF16)
        s = _dot_nt(qe, k)
        p = jnp.exp(s - jnp.max(s, axis=-1, keepdims=True))
        outs.append(_dot(p.astype(BF16), v) / jnp.sum(p, axis=-1, keepdims=True))
    o_ref[0, 0] = jnp.where(lane < HEAD_DIM, outs[0], outs[1])


def _mix_b_ctx_call(zc):
    b, _, cl, _ = zc.shape
    blk = lambda t: pl.BlockSpec((1, 1, cl, LANES), lambda i, p: (i, t + p, 0, 0))
    return pl.pallas_call(
        _mix_b_ctx_kernel,
        grid=(b, N_PAIRS),
        in_specs=[blk(T_BQ), blk(T_BK), blk(T_BV)],
        out_specs=blk(0),
        out_shape=jax.ShapeDtypeStruct((b, N_PAIRS, cl, LANES), F32),
        compiler_params=_cparams(("arbitrary", "arbitrary")),
        name="mix_b_ctx",
    )(zc, zc, zc)


def _natten_bias(rpb, n_rows):
    h = rpb.shape[0]
    padded = jnp.pad(rpb, ((0, 0), (0, 0), (GRID_W, GRID_W)))
    toe = jnp.stack([padded[:, :, GRID_W + WIN_COLS - 1 - c:2 * GRID_W + WIN_COLS - 1 - c] for c in range(GRID_W)], axis=2)
    c = np.arange(GRID_W)[:, None]
    kc = np.arange(GRID_W)[None, :]
    cs = np.clip(c - WIN_COLS // 2, 0, GRID_W - WIN_COLS)
    col_ok = (kc >= cs) & (kc < cs + WIN_COLS)
    toe = jnp.where(col_ok[None, None], toe, NEG)
    neg_block = jnp.full((h, GRID_W, GRID_W), NEG, F32)
    classes = []
    for r0 in (0, Q_ROWS, n_rows - Q_ROWS):
        k0 = int(np.clip(r0 - WIN_ROWS // 2, 0, n_rows - K_ROWS))
        q_rows = []
        for i in range(Q_ROWS):
            r = r0 + i
            rs = int(np.clip(r - WIN_ROWS // 2, 0, n_rows - WIN_ROWS))
            blocks = []
            for j in range(K_ROWS):
                kr = k0 + j
                blocks.append(toe[:, kr - r + WIN_ROWS - 1] if rs <= kr < rs + WIN_ROWS else neg_block)
            q_rows.append(jnp.concatenate(blocks, axis=-1))
        classes.append(jnp.concatenate(q_rows, axis=1))
    bias = jnp.stack(classes, axis=1)
    return bias.reshape(h // 2, 2, 3, Q_ROWS * GRID_W, K_ROWS * GRID_W)


def _mlstm_chunks(items, dmask_ref, head0, ones_blk, bd_mask):
    n_l = items[0][1].shape[0]
    half = n_l // 2
    zeros = jnp.zeros((half, half), BF16)

    early = []
    for (direction, qb, q2, kt, v, v_heads, fc, w, b_rows, state) in items:
        st, m_pl = state
        s_heads = _dot(q2, kt)
        inter = _dot(qb, st.astype(BF16))
        w_max = jnp.max(w, axis=0, keepdims=True)
        a_s = jnp.exp2(w - w_max)
        upd = jnp.where(bd_mask, _dot(kt, jnp.concatenate([a_s * v, a_s], axis=1).astype(BF16)), 0.0)
        f_tot = fc[n_l - 1:n_l, :] if direction == 0 else fc[0:1, :]
        mw = jnp.maximum(m_pl, w_max)
        a_prev = jnp.exp2(m_pl - mw)
        a_new = jnp.exp2(w_max - mw)
        st_new = jnp.concatenate([a_prev, a_prev], axis=1) * st + jnp.concatenate([a_new, a_new], axis=1) * upd
        early.append((s_heads, inter, m_pl, st_new, f_tot + mw))

    weights = []
    for (direction, qb, q2, kt, v, v_heads, fc, w, b_rows, state), (s_heads, _, _, _, _) in zip(items, early):
        first = slice(0, half) if direction == 0 else slice(half, n_l)
        last = slice(half, n_l) if direction == 0 else slice(0, half)
        p_heads, cm_heads = [], []
        for e in range(2):
            s = s_heads[e * n_l:(e + 1) * n_l]
            bm_a = b_rows[e][:, first] + dmask_ref[direction, first, first]
            bm_b = b_rows[e] + dmask_ref[direction, last, :]
            cm_a = jnp.max(bm_a, axis=-1, keepdims=True)
            cm_b = jnp.max(bm_b, axis=-1, keepdims=True)
            p_a = (s[first, first] * jnp.exp2(bm_a - cm_a)).astype(BF16)
            p_b = (s[last, :] * jnp.exp2(bm_b - cm_b)).astype(BF16)
            if direction == 0:
                p_heads.append(jnp.concatenate([jnp.concatenate([p_a, zeros], axis=1), p_b], axis=0))
                cm_heads.append(jnp.concatenate([cm_a, cm_b], axis=0))
            else:
                p_heads.append(jnp.concatenate([p_b, jnp.concatenate([zeros, p_a], axis=1)], axis=0))
                cm_heads.append(jnp.concatenate([cm_b, cm_a], axis=0))
        weights.append((jnp.concatenate(p_heads, axis=1), jnp.where(head0, cm_heads[0], cm_heads[1])))

    intras = [_dot(p, jnp.concatenate([item[5], ones_blk], axis=1)) for item, (p, _) in zip(items, weights)]

    results = []
    for item, (_, inter, m_pl, st_new, m_new), (_, cm_pl), intra in zip(items, early, weights, intras):
        fc = item[6]
        mx = jnp.maximum(m_pl, cm_pl)
        w_intra = jnp.exp2(cm_pl - mx)
        w_inter = jnp.exp2(m_pl - mx)
        num = w_inter * inter[:, :LANES] + w_intra * intra[:, :LANES]
        den = w_inter * inter[:, LANES:] + w_intra * intra[:, LANES:]
        results.append((num / jnp.maximum(jnp.abs(den), jnp.exp2(-(fc + mx))), st_new, m_new))
    return results


def _mix_c_kernel(zq_ref, zk_ref, zv_ref, zo_ref, rg_ref, cq_ref, ck_ref, cv_ref, co_ref, crg_ref,
                  wq_ref, wk_ref, bq_ref, bk_ref, cos_ref, sin_ref, esel_ref, tri_ref, dmask_ref,
                  o_ref, oc_ref, qs_ref, q2_ref, kt_ref, vh_ref, fw_ref, br_ref):
    s_len = zq_ref.shape[2]
    n_l = C_LEN
    n_chunks = s_len // n_l
    lane = lax.broadcasted_iota(jnp.int32, (n_l, LANES), 1)
    row = lax.broadcasted_iota(jnp.int32, (n_l, LANES), 0)
    k_scale = HEAD_DIM ** -0.5

    def conv_silu(x_ref, t0, total, w_ref_, b_ref_):
        x = x_ref[0, 0, pl.ds(t0, n_l), :]
        if total == n_l:
            prev_row = next_row = jnp.zeros((1, LANES), F32)
        else:
            prev_row = x_ref[0, 0, pl.ds(jnp.maximum(t0 - 1, 0), 1), :] * (t0 > 0).astype(F32)
            next_row = x_ref[0, 0, pl.ds(jnp.minimum(t0 + n_l, total - 1), 1), :] * (t0 + n_l < total).astype(F32)
        x_prev = jnp.where(row == 0, prev_row, pltpu.roll(x, 1, axis=0))
        x_next = jnp.where(row == n_l - 1, next_row, pltpu.roll(x, n_l - 1, axis=0))
        y = x_prev * w_ref_[0, 0:1, :] + x * w_ref_[0, 1:2, :] + x_next * w_ref_[0, 2:3, :] + b_ref_[0]
        return _silu(y)

    def rope(x, t0):
        cos = cos_ref[pl.ds(t0, n_l), :]
        sin = sin_ref[pl.ds(t0, n_l), :]
        first = (lane % (HEAD_DIM // 2)) < (HEAD_DIM // 4)
        partner = jnp.where(first, pltpu.roll(x, LANES - HEAD_DIM // 4, axis=1), pltpu.roll(x, HEAD_DIM // 4, axis=1))
        return x * cos + partner * sin

    def qk_prep(q_in, k_in, t0, total, with_rope):
        q = conv_silu(q_in, t0, total, wq_ref, bq_ref)
        k = conv_silu(k_in, t0, total, wk_ref, bk_ref)
        if with_rope:
            q, k = rope(q, t0), rope(k, t0)
        return q, (k * k_scale).T.astype(BF16)

    def gate_prep(rg, d):
        n_g = rg.shape[0]
        f3 = _dot(_stack3(rg), tri_ref[1 - d])
        f_row = f3[:n_g] + f3[n_g:2 * n_g] + f3[2 * n_g:]
        b_rows = [rg[2 * d + e:2 * d + e + 1, :] - f_row[4 + 2 * d + e:5 + 2 * d + e, :] for e in range(2)]
        fw = lax.dot_general(_stack3(jnp.concatenate([f_row, rg], axis=0)), esel_ref[d], (((0,), (0,)), ((), ())),
                             preferred_element_type=F32)
        return fw[:, :LANES], fw[:, LANES:], b_rows

    head0 = lane < HEAD_DIM

    def per_head(x):
        return jnp.concatenate([jnp.where(head0, x, 0.0), jnp.where(head0, 0.0, x)], axis=0).astype(BF16)

    ones_blk = jnp.concatenate([jnp.where(head0, 1.0, 0.0), jnp.where(head0, 0.0, 1.0)], axis=0).astype(BF16)
    bd_r = lax.broadcasted_iota(jnp.int32, (LANES, 2 * LANES), 0)
    bd_c = lax.broadcasted_iota(jnp.int32, (LANES, 2 * LANES), 1)
    bd_mask = (bd_r < HEAD_DIM) == ((bd_c % LANES) < HEAD_DIM)
    consts = (head0, ones_blk, bd_mask)

    cq, ckt = qk_prep(cq_ref, ck_ref, 0, n_l, False)
    cv = cv_ref[0, 0]
    crg = crg_ref[0, 0] * LOG2E
    items = []
    for d in range(2):
        items.append((d, cq.astype(BF16), per_head(cq), ckt, cv, per_head(cv), *gate_prep(crg, d),
                      (jnp.zeros((LANES, 2 * LANES), F32), jnp.zeros((1, LANES), F32))))
    res = _mlstm_chunks(items, dmask_ref, *consts)
    oc_ref[0, 0] = (res[0][0] + res[1][0]) * _sigmoid(co_ref[0, 0])
    carry = [(st, m_pl) for _, st, m_pl in res]

    def prep(c, carry):
        t0 = pl.multiple_of(c * n_l, n_l)
        rows = pl.ds(t0, n_l)
        q, kt_ref[:, rows] = qk_prep(zq_ref, zk_ref, t0, s_len, True)
        qs_ref[rows, :] = q.astype(BF16)
        q2_ref[:, rows, :] = per_head(q).reshape(2, n_l, LANES)
        vh_ref[:, rows, :] = per_head(zv_ref[0, 0, rows, :]).reshape(2, n_l, LANES)
        rg = rg_ref[0, 0, :, rows] * LOG2E
        for d in range(2):
            fc, w, b_rows = gate_prep(rg, d)
            fw_ref[d, rows, :] = jnp.concatenate([fc, w], axis=1)
            for e in range(2):
                br_ref[2 * d + e:2 * d + e + 1, rows] = b_rows[e]
        return carry

    lax.fori_loop(0, n_chunks, prep, 0, unroll=4)

    def make_step(second_visit):
        def step(i, carry):
            items, all_rows = [], []
            for d in range(2):
                c = i if d == 0 else n_chunks - 1 - i
                rows = pl.ds(pl.multiple_of(c * n_l, n_l), n_l)
                fw = fw_ref[d, rows, :]
                b_rows = [br_ref[2 * d + e:2 * d + e + 1, rows] for e in range(2)]
                items.append((d, qs_ref[rows, :], q2_ref[:, rows, :].reshape(2 * n_l, LANES), kt_ref[:, rows],
                              zv_ref[0, 0, rows, :], vh_ref[:, rows, :].reshape(2 * n_l, LANES),
                              fw[:, :LANES], fw[:, LANES:], b_rows, carry[d]))
                all_rows.append(rows)
            res = _mlstm_chunks(items, dmask_ref, *consts)
            for rows, (h, _, _) in zip(all_rows, res):
                if second_visit:
                    o_ref[0, 0, rows, :] = (o_ref[0, 0, rows, :] + h) * _sigmoid(zo_ref[0, 0, rows, :])
                else:
                    o_ref[0, 0, rows, :] = h
            return tuple((st, m_pl) for _, st, m_pl in res)
        return step

    carry = lax.fori_loop(0, n_chunks // 2, make_step(False), tuple(carry), unroll=2)
    lax.fori_loop(n_chunks // 2, n_chunks, make_step(True), carry, unroll=2)


def _mix_c_call(z, zg, zc, zcg, wconv, bconv, cos_t, sin_t, esel, tri, dmask):
    b, _, s, _ = z.shape
    cl = zc.shape[2]
    assert cl == C_LEN and s % (4 * C_LEN) == 0
    once = dict(pipeline_mode=pl.Buffered(1))
    zt = lambda t: pl.BlockSpec((1, 1, s, LANES), lambda i, p: (i, t + p, 0, 0))
    ct = lambda t: pl.BlockSpec((1, 1, cl, LANES), lambda i, p: (i, t + p, 0, 0))
    return pl.pallas_call(
        _mix_c_kernel,
        grid=(b, N_PAIRS),
        in_specs=[zt(T_CQ), zt(T_CK), zt(T_CV), zt(T_CO),
                  pl.BlockSpec((1, 1, PAIR_GATES, s), lambda i, p: (i, p, 0, 0)),
                  ct(T_CQ), ct(T_CK), ct(T_CV), ct(T_CO),
                  pl.BlockSpec((1, 1, PAIR_GATES, cl), lambda i, p: (i, p, 0, 0)),
                  pl.BlockSpec((1, 3, LANES), lambda i, p: (p, 0, 0)),
                  pl.BlockSpec((1, 3, LANES), lambda i, p: (N_PAIRS + p, 0, 0)),
                  pl.BlockSpec((1, 1, LANES), lambda i, p: (p, 0, 0)),
                  pl.BlockSpec((1, 1, LANES), lambda i, p: (N_PAIRS + p, 0, 0)),
                  pl.BlockSpec((s, LANES), lambda i, p: (0, 0), **once),
                  pl.BlockSpec((s, LANES), lambda i, p: (0, 0), **once),
                  pl.BlockSpec((2, 48, 2 * LANES), lambda i, p: (0, 0, 0), **once),
                  pl.BlockSpec((2, C_LEN, C_LEN), lambda i, p: (0, 0, 0), **once),
                  pl.BlockSpec((2, C_LEN, C_LEN), lambda i, p: (0, 0, 0), **once)],
        out_specs=[pl.BlockSpec((1, 1, s, LANES), lambda i, p: (i, p, 0, 0)),
                   pl.BlockSpec((1, 1, cl, LANES), lambda i, p: (i, p, 0, 0))],
        out_shape=[jax.ShapeDtypeStruct((b, N_PAIRS, s, LANES), F32),
                   jax.ShapeDtypeStruct((b, N_PAIRS, cl, LANES), F32)],
        scratch_shapes=[pltpu.VMEM((s, LANES), BF16), pltpu.VMEM((2, s, LANES), BF16), pltpu.VMEM((LANES, s), BF16),
                        pltpu.VMEM((2, s, LANES), BF16), pltpu.VMEM((2, s, 2 * LANES), F32), pltpu.VMEM((PAIR_GATES, s), F32)],
        compiler_params=_cparams(("arbitrary", "arbitrary")),
        name="mix_c_mlstm",
    )(z, z, z, z, zg.reshape(b, N_PAIRS, PAIR_GATES, s), zc, zc, zc, zc, zcg.reshape(b, N_PAIRS, PAIR_GATES, cl),
      wconv, wconv, bconv, bconv, cos_t, sin_t, esel, tri, dmask)


def _rope_tables(s):
    t = np.arange(s)
    quarter = HEAD_DIM // 4
    inv_freq = (ROPE_BASE ** (-np.arange(quarter, dtype=np.float32) / quarter)).astype(np.float32)
    lane = np.arange(LANES) % HEAD_DIM
    use_col = lane >= HEAD_DIM // 2
    second = (lane % (HEAD_DIM // 2)) >= quarter
    pos = np.where(use_col[None, :], (t % GRID_W)[:, None], (t // GRID_W)[:, None]).astype(np.float32)
    ang = (pos * inv_freq[lane % quarter][None, :]).astype(np.float64)
    sin = np.where(second[None, :], np.sin(ang), -np.sin(ang))
    return jnp.asarray(np.cos(ang), F32), jnp.asarray(sin, F32)


def _gate_tables():
    esel = np.zeros((2, 48, 2 * LANES), np.float32)
    for d in range(2):
        for ln in range(LANES):
            e = ln // HEAD_DIM
            esel[d, 4 + 2 * d + e, ln] = 1.0
            esel[d, 4 + 2 * d + e, LANES + ln] = -1.0
            esel[d, 8 + 2 * d + e, LANES + ln] = 1.0
    esel[:, 16:32, :] = esel[:, :16, :]
    esel[:, 32:, :] = esel[:, :16, :]
    t = np.arange(C_LEN)
    tri = np.stack([(t[None, :] <= t[:, None]), (t[None, :] >= t[:, None])])
    dmask = np.where(tri, 0.0, NEG).astype(np.float32)
    return jnp.asarray(esel, BF16), jnp.asarray(tri.astype(np.float32), BF16), jnp.asarray(dmask)


def _post_kernel(x_ref, ya_ref, yb_ref, yc_ref, mod_ref, wout_ref, g_ref, wg_ref, wu_ref, wd_ref, fg_ref, o_ref,
                 *, final, ff_chunk):
    y = jnp.concatenate([ya_ref[0, 0], ya_ref[0, 1], yb_ref[0, 0], yb_ref[0, 1], yb_ref[0, 2],
                         yc_ref[0, 0], yc_ref[0, 1], yc_ref[0, 2]], axis=1).astype(BF16)
    x1 = x_ref[0] + mod_ref[0, 2:3, :] * _dot(y, wout_ref[...])
    hm = _rms_modulate(x1, g_ref[...], mod_ref[0, 3:4, :], mod_ref[0, 4:5, :]).astype(BF16)
    acc = jnp.zeros_like(x1)
    for c in range(wg_ref.shape[1] // ff_chunk):
        cols = slice(c * ff_chunk, (c + 1) * ff_chunk)
        a = _silu(_dot(hm, wg_ref[:, cols])) * _dot(hm, wu_ref[:, cols])
        acc = acc + _dot(a.astype(BF16), wd_ref[cols, :])
    x2 = x1 + mod_ref[0, 5:6, :] * acc
    if final:
        ms = jnp.mean(x2 * x2, axis=-1, keepdims=True)
        x2 = x2 * lax.rsqrt(ms + EPS) * fg_ref[...]
    o_ref[0] = x2


def _post_call(x, ya, yb, yc, mod, wout, g, wg, wu, wd, fg, tm, final):
    b, s, d = x.shape
    dff = wg.shape[1]
    per_sample = mod.shape[0] == b
    mod_map = (lambda i, j: (i, 0, 0)) if per_sample else (lambda i, j: (0, 0, 0))
    const = lambda i, j: (0, 0)
    once = dict(pipeline_mode=pl.Buffered(1))
    yt = lambda n: pl.BlockSpec((1, n, tm, LANES), lambda i, j: (i, 0, j, 0))
    return pl.pallas_call(
        functools.partial(_post_kernel, final=final, ff_chunk=MXU_N),
        grid=(b, s // tm),
        in_specs=[pl.BlockSpec((1, tm, d), lambda i, j: (i, j, 0)),
                  yt(2), yt(N_PAIRS), yt(N_PAIRS),
                  pl.BlockSpec((1, N_MOD, d), mod_map),
                  pl.BlockSpec((d, d), const, **once),
                  pl.BlockSpec((1, d), const),
                  pl.BlockSpec((d, dff), const, **once),
                  pl.BlockSpec((d, dff), const, **once),
                  pl.BlockSpec((dff, d), const, **once),
                  pl.BlockSpec((1, d), const)],
        out_specs=pl.BlockSpec((1, tm, d), lambda i, j: (i, j, 0)),
        out_shape=jax.ShapeDtypeStruct((b, s, d), F32),
        compiler_params=_cparams(("arbitrary", "arbitrary")),
        name="out_proj_swiglu",
    )(x, ya, yb, yc, mod, wout, g, wg, wu, wd, fg)


def _gate_perm():
    row = np.zeros(N_GATES, np.int32)
    for p in range(N_PAIRS):
        for k in range(4):
            for e in range(2):
                row[PAIR_GATES * p + 4 * (k % 2) + 2 * (k // 2) + e] = C_HEADS * k + 2 * p + e
    return row


def kernel(x, c, ctx, c_ctx, w_mod, b_mod, norm1_g, w_in, a_ln_g, a_ln_b, a_ws, a_bs, b_rpb,
           c_conv_w, c_conv_b, c_gate_b, w_out, norm2_g, w_gate, w_up, w_down, final_g):
    b, s, d = x.shape
    cl = ctx.shape[1]
    depth = w_mod.shape[0]
    n_main = (A_TILES + N_TILES) * LANES
    tm = 512
    perm = _gate_perm()

    pad = (-(b + 1)) % 8
    cs = jnp.concatenate([c, c_ctx[None, :], jnp.zeros((pad, d), F32)], axis=0)
    mod_all = _mod_call(cs, w_mod.astype(BF16), b_mod)

    cos_t, sin_t = _rope_tables(s)
    esel, tri, dmask = _gate_tables()
    bias_all = _natten_bias(b_rpb.reshape((-1,) + b_rpb.shape[2:]) * LOG2E, s // GRID_W)
    avg = np.kron(np.eye(2), np.full((HEAD_DIM, HEAD_DIM), 1.0 / HEAD_DIM))
    avg = jnp.asarray(np.concatenate([avg, avg], axis=0), BF16)

    xc = ctx
    for l in range(depth):
        last = l == depth - 1
        mod = mod_all[l, :b].reshape(b, N_MOD, d)
        mod_c = mod_all[l, b:b + 1].reshape(1, N_MOD, d)

        w_main = jnp.concatenate([w_in[l][:, :n_main], w_in[l][:, n_main:][:, perm],
                                  jnp.zeros((d, LANES - N_GATES), F32)], axis=1).astype(BF16)
        gbt = jnp.broadcast_to(c_gate_b[l].reshape(-1)[perm][:, None], (N_GATES, tm))
        g1 = norm1_g[l][None, :]

        mix_a = (a_ln_g[l].reshape(2, 1, LANES), a_ln_b[l].reshape(2, 1, LANES), a_ws[l].astype(BF16),
                 jnp.repeat(a_bs[l].reshape(2, 2, A_CHUNK).transpose(0, 2, 1), HEAD_DIM, axis=2), avg)
        z, zg, ya = _inproj_call(x, mod, g1, w_main, gbt, *mix_a, tm)
        zc, zcg, yac = _inproj_call(xc, mod_c, g1, w_main, gbt, *mix_a, cl)

        yb = _mix_b_call(z, zc, bias_all, l)

        wconv = c_conv_w[l].reshape(3, 2 * N_PAIRS, LANES).transpose(1, 0, 2)
        bconv = c_conv_b[l].reshape(2 * N_PAIRS, 1, LANES)
        yc, ycc = _mix_c_call(z, zg, zc, zcg, wconv, bconv, cos_t, sin_t, esel, tri, dmask)

        weights = (w_out[l].astype(BF16), norm2_g[l][None, :], w_gate[l].astype(BF16), w_up[l].astype(BF16),
                   w_down[l].astype(BF16), final_g[None, :])
        x = _post_call(x, ya, yb, yc, mod, *weights, tm, last)
        if not last:
            ybc = _mix_b_ctx_call(zc)
            xc = _post_call(xc, yac, ybc, ycc, mod_c, *weights, cl, False)
    return x
```

```python
import functools

import numpy as np
import jax
import jax.numpy as jnp
from jax import lax
from jax.experimental import pallas as pl
from jax.experimental.pallas import tpu as pltpu

F32 = jnp.float32
BF16 = jnp.bfloat16

LANES = 128
MXU_N = 256
HEAD_DIM = 64
GRID_W = 64
A_GROUPS = 4
A_CHUNK = 128
B_HEADS = 6
C_HEADS = 6
WIN_ROWS = 8
WIN_COLS = 16
ROPE_BASE = 10000.0
EPS = 1e-6
N_MOD = 6
NEG = -1e30
LOG2E = float(np.log2(np.e))

A_TILES = 4
T_BQ, T_BK, T_BV, T_CQ, T_CK, T_CV, T_CO = 0, 3, 6, 9, 12, 15, 18
N_TILES = 21
N_PAIRS = 3
N_GATES = 4 * C_HEADS
PAIR_GATES = N_GATES // N_PAIRS
Q_ROWS = 4
K_ROWS = Q_ROWS + WIN_ROWS
C_LEN = 256

VMEM_LIMIT = 56 * 1024 * 1024


def _cparams(sem):
    return pltpu.CompilerParams(dimension_semantics=sem, vmem_limit_bytes=VMEM_LIMIT)


def _dot(a, b):
    return jnp.dot(a, b, preferred_element_type=F32)


def _dot_nt(a, b):
    return lax.dot_general(a, b, (((1,), (1,)), ((), ())), preferred_element_type=F32)


def _split2(x):
    h = x.astype(BF16)
    return h, (x - h.astype(F32)).astype(BF16)


def _dot2_r(x, sel2):
    h, l = _split2(x)
    return _dot(jnp.concatenate([h, l], axis=1), sel2)


def _stack3(x):
    h = x.astype(BF16).astype(F32)
    r = x - h
    m = r.astype(BF16).astype(F32)
    return jnp.concatenate([h, m, r - m], axis=0).astype(BF16)


def _silu(x):
    h = 0.5 * x
    return h + h * jnp.tanh(h)


def _sigmoid(x):
    return 0.5 + 0.5 * jnp.tanh(0.5 * x)


def _gelu_tanh(x):
    return 0.5 * x * (1.0 + jnp.tanh(np.sqrt(2.0 / np.pi).astype(np.float32) * (x + 0.044715 * (x * x * x))))


def _log_sigmoid(x):
    return jnp.minimum(x, 0.0) - jnp.log1p(jnp.exp(-jnp.abs(x)))


def _rms_modulate(x, g, shift, scale):
    ms = jnp.mean(x * x, axis=-1, keepdims=True)
    return (x * lax.rsqrt(ms + EPS) * g) * (1.0 + scale) + shift


def _mod_kernel(c_ref, w_ref, b_ref, o_ref):
    o_ref[0] = _dot(_silu(c_ref[...]).astype(BF16), w_ref[0]) + b_ref[0, 0]


def _mod_call(cs, w_mod, b_mod):
    depth, d, nd = w_mod.shape
    rows = cs.shape[0]
    return pl.pallas_call(
        _mod_kernel,
        grid=(depth, nd // d),
        in_specs=[pl.BlockSpec((rows, d), lambda l, n: (0, 0)),
                  pl.BlockSpec((1, d, d), lambda l, n: (l, 0, n)),
                  pl.BlockSpec((1, 1, 1, d), lambda l, n: (l, n, 0, 0))],
        out_specs=pl.BlockSpec((1, rows, d), lambda l, n: (l, 0, n)),
        out_shape=jax.ShapeDtypeStruct((depth, rows, nd), F32),
        compiler_params=_cparams(("arbitrary", "arbitrary")),
        name="adaln_mod",
    )(cs, w_mod, b_mod.reshape(depth, nd // d, 1, d))


def _gmlp(u, v, j, lng_ref, lnb_ref, ws_ref, bsx_ref, avg):
    lane = lax.broadcasted_iota(jnp.int32, (A_CHUNK, LANES), 1)
    v = _gelu_tanh(v)
    dv = v - _dot2_r(v, avg)
    var = _dot2_r(dv * dv, avg)
    vn = (dv * lax.rsqrt(var + EPS) * lng_ref[j] + lnb_ref[j]).astype(BF16)
    u = _gelu_tanh(u)
    outs = []
    for c in range(u.shape[0] // A_CHUNK):
        rows = slice(c * A_CHUNK, (c + 1) * A_CHUNK)
        s = jnp.where(lane < HEAD_DIM, _dot(ws_ref[2 * j], vn[rows]), _dot(ws_ref[2 * j + 1], vn[rows])) + bsx_ref[j]
        outs.append(u[rows] * s)
    return jnp.concatenate(outs, axis=0)


def _inproj_kernel(x_ref, mod_ref, g_ref, w_ref, gbt_ref, lng_ref, lnb_ref, ws_ref, bsx_ref, avg_ref,
                   z_ref, zgt_ref, ya_ref):
    xm = _rms_modulate(x_ref[0], g_ref[...], mod_ref[0, 0:1, :], mod_ref[0, 1:2, :])
    xb = xm.astype(BF16)
    u = _dot(xb, w_ref[:, :MXU_N])
    v = _dot(xb, w_ref[:, MXU_N:2 * MXU_N])
    for j in range((N_TILES + 1) // 2):
        r = _dot(xb, w_ref[:, (A_TILES // 2 + j) * MXU_N:(A_TILES // 2 + j + 1) * MXU_N])
        z_ref[0, 2 * j] = r[:, :LANES]
        if 2 * j + 1 < N_TILES:
            z_ref[0, 2 * j + 1] = r[:, LANES:]
    g = r[:, LANES:].T[:N_GATES, :] + gbt_ref[...]
    sub = lax.broadcasted_iota(jnp.int32, g.shape, 0)
    zgt_ref[0] = jnp.where(sub % PAIR_GATES >= PAIR_GATES // 2, _log_sigmoid(g), g)
    for j in range(2):
        cols = slice(j * LANES, (j + 1) * LANES)
        ya_ref[0, j] = _gmlp(u[:, cols], v[:, cols], j, lng_ref, lnb_ref, ws_ref, bsx_ref, avg_ref[...])


def _inproj_call(x, mod, g, w, gbt, lng, lnb, ws, bsx, avg, tm):
    b, s, d = x.shape
    per_sample = mod.shape[0] == b
    mod_map = (lambda i, j: (i, 0, 0)) if per_sample else (lambda i, j: (0, 0, 0))
    const = lambda i, j: (0, 0)
    c3 = lambda i, j: (0, 0, 0)
    return pl.pallas_call(
        _inproj_kernel,
        grid=(b, s // tm),
        in_specs=[pl.BlockSpec((1, tm, d), lambda i, j: (i, j, 0)),
                  pl.BlockSpec((1, N_MOD, d), mod_map),
                  pl.BlockSpec((1, d), const),
                  pl.BlockSpec((d, (A_TILES + N_TILES + 1) * LANES), const, pipeline_mode=pl.Buffered(1)),
                  pl.BlockSpec((N_GATES, tm), const),
                  pl.BlockSpec((2, 1, LANES), c3),
                  pl.BlockSpec((2, 1, LANES), c3),
                  pl.BlockSpec((A_GROUPS, A_CHUNK, A_CHUNK), c3),
                  pl.BlockSpec((2, A_CHUNK, LANES), c3),
                  pl.BlockSpec((2 * LANES, LANES), const)],
        out_specs=[pl.BlockSpec((1, N_TILES, tm, LANES), lambda i, j: (i, 0, j, 0)),
                   pl.BlockSpec((1, N_GATES, tm), lambda i, j: (i, 0, j)),
                   pl.BlockSpec((1, 2, tm, LANES), lambda i, j: (i, 0, j, 0))],
        out_shape=[jax.ShapeDtypeStruct((b, N_TILES, s, LANES), F32),
                   jax.ShapeDtypeStruct((b, N_GATES, s), F32),
                   jax.ShapeDtypeStruct((b, 2, s, LANES), F32)],
        compiler_params=_cparams(("arbitrary", "arbitrary")),
        name="in_proj",
    )(x, mod, g, w, gbt[:, :tm], lng, lnb, ws, bsx, avg)


def _mix_b_kernel(q_ref, k_ref, v_ref, kc_ref, vc_ref, bias_ref, o_ref, kb_ref, vb_ref, *, n_rows):
    n_blocks = n_rows // Q_ROWS
    nq = Q_ROWS * GRID_W
    nk = K_ROWS * GRID_W
    kb_ref[...] = k_ref[0, 0].astype(BF16)
    vb_ref[...] = v_ref[0, 0].astype(BF16)
    kc = kc_ref[0, 0].astype(BF16)
    vc = vc_ref[0, 0].astype(BF16)
    lane = lax.broadcasted_iota(jnp.int32, (nq, LANES), 1)

    def block_pair(i, carry):
        units = []
        for rb in (2 * i, 2 * i + 1):
            q_start = pl.multiple_of(rb * nq, nq)
            k_start = pl.multiple_of(jnp.clip(rb * Q_ROWS - WIN_ROWS // 2, 0, n_rows - K_ROWS) * GRID_W, nq)
            cls = jnp.where(rb == 0, 0, jnp.where(rb == n_blocks - 1, 2, 1))
            q = q_ref[0, 0, pl.ds(q_start, nq), :] * (HEAD_DIM ** -0.5 * LOG2E)
            kw = kb_ref[pl.ds(k_start, nk), :]
            for e in range(2):
                qe = jnp.where((lane < HEAD_DIM) == (e == 0), q, 0.0).astype(BF16)
                units.append((q_start, k_start, _dot_nt(qe, kw) + bias_ref[0, e, cls], _dot_nt(qe, kc)))
        probs = []
        for _, _, s_w, s_c in units:
            m = jnp.maximum(jnp.max(s_w, axis=-1, keepdims=True), jnp.max(s_c, axis=-1, keepdims=True))
            p_w = jnp.exp2(s_w - m)
            p_c = jnp.exp2(s_c - m)
            den = jnp.sum(p_w, axis=-1, keepdims=True) + jnp.sum(p_c, axis=-1, keepdims=True)
            probs.append((p_w.astype(BF16), p_c.astype(BF16), den))
        outs = [(_dot(p_w, vb_ref[pl.ds(k_start, nk), :]) + _dot(p_c, vc)) / den
                for (_, k_start, _, _), (p_w, p_c, den) in zip(units, probs)]
        for b2 in range(2):
            o_ref[0, 0, pl.ds(units[2 * b2][0], nq), :] = jnp.where(lane < HEAD_DIM, outs[2 * b2], outs[2 * b2 + 1])
        return carry

    lax.fori_loop(0, n_blocks // 2, block_pair, 0)


def _mix_b_call(z, zc, bias, layer):
    b, _, s, _ = z.shape
    cl = zc.shape[2]
    n_rows = s // GRID_W
    assert n_rows % (2 * Q_ROWS) == 0 and n_rows >= K_ROWS
    zt = lambda t: pl.BlockSpec((1, 1, s, LANES), lambda p, i: (i, t + p, 0, 0))
    ct = lambda t: pl.BlockSpec((1, 1, cl, LANES), lambda p, i: (i, t + p, 0, 0))
    return pl.pallas_call(
        functools.partial(_mix_b_kernel, n_rows=n_rows),
        grid=(N_PAIRS, b),
        in_specs=[zt(T_BQ), zt(T_BK), zt(T_BV), ct(T_BK), ct(T_BV),
                  pl.BlockSpec((1, 2, 3, Q_ROWS * GRID_W, K_ROWS * GRID_W), lambda p, i: (layer * N_PAIRS + p, 0, 0, 0, 0))],
        out_specs=pl.BlockSpec((1, 1, s, LANES), lambda p, i: (i, p, 0, 0)),
        out_shape=jax.ShapeDtypeStruct((b, N_PAIRS, s, LANES), F32),
        scratch_shapes=[pltpu.VMEM((s, LANES), BF16), pltpu.VMEM((s, LANES), BF16)],
        compiler_params=_cparams(("arbitrary", "arbitrary")),
        name="mix_b_natten",
    )(z, z, z, zc, zc, bias)


def _mix_b_ctx_kernel(q_ref, k_ref, v_ref, o_ref):
    q = q_ref[0, 0] * (HEAD_DIM ** -0.5)
    k = k_ref[0, 0].astype(BF16)
    v = v_ref[0, 0].astype(BF16)
    lane = lax.broadcasted_iota(jnp.int32, q.shape, 1)
    outs = []
    for e in range(2):
        qe = jnp.where((lane < HEAD_DIM) == (e == 0), q, 0.0).astype(BF16)
        s = _dot_nt(qe, k)
        p = jnp.exp(s - jnp.max(s, axis=-1, keepdims=True))
        outs.append(_dot(p.astype(BF16), v) / jnp.sum(p, axis=-1, keepdims=True))
    o_ref[0, 0] = jnp.where(lane < HEAD_DIM, outs[0], outs[1])


def _mix_b_ctx_call(zc):
    b, _, cl, _ = zc.shape
    blk = lambda t: pl.BlockSpec((1, 1, cl, LANES), lambda i, p: (i, t + p, 0, 0))
    return pl.pallas_call(
        _mix_b_ctx_kernel,
        grid=(b, N_PAIRS),
        in_specs=[blk(T_BQ), blk(T_BK), blk(T_BV)],
        out_specs=blk(0),
        out_shape=jax.ShapeDtypeStruct((b, N_PAIRS, cl, LANES), F32),
        compiler_params=_cparams(("arbitrary", "arbitrary")),
        name="mix_b_ctx",
    )(zc, zc, zc)


def _natten_bias(rpb, n_rows):
    h = rpb.shape[0]
    padded = jnp.pad(rpb, ((0, 0), (0, 0), (GRID_W, GRID_W)))
    toe = jnp.stack([padded[:, :, GRID_W + WIN_COLS - 1 - c:2 * GRID_W + WIN_COLS - 1 - c] for c in range(GRID_W)], axis=2)
    c = np.arange(GRID_W)[:, None]
    kc = np.arange(GRID_W)[None, :]
    cs = np.clip(c - WIN_COLS // 2, 0, GRID_W - WIN_COLS)
    col_ok = (kc >= cs) & (kc < cs + WIN_COLS)
    toe = jnp.where(col_ok[None, None], toe, NEG)
    neg_block = jnp.full((h, GRID_W, GRID_W), NEG, F32)
    classes = []
    for r0 in (0, Q_ROWS, n_rows - Q_ROWS):
        k0 = int(np.clip(r0 - WIN_ROWS // 2, 0, n_rows - K_ROWS))
        q_rows = []
        for i in range(Q_ROWS):
            r = r0 + i
            rs = int(np.clip(r - WIN_ROWS // 2, 0, n_rows - WIN_ROWS))
            blocks = []
            for j in range(K_ROWS):
                kr = k0 + j
                blocks.append(toe[:, kr - r + WIN_ROWS - 1] if rs <= kr < rs + WIN_ROWS else neg_block)
            q_rows.append(jnp.concatenate(blocks, axis=-1))
        classes.append(jnp.concatenate(q_rows, axis=1))
    bias = jnp.stack(classes, axis=1)
    return bias.reshape(h // 2, 2, 3, Q_ROWS * GRID_W, K_ROWS * GRID_W)


def _mlstm_chunks(items, dmask_ref, head0, ones_blk, bd_mask):
    n_l = items[0][1].shape[0]
    half = n_l // 2
    zeros = jnp.zeros((half, half), BF16)

    early = []
    for (direction, qb, q2, kt, v, v_heads, fc, w, b_rows, state) in items:
        st, m_pl = state
        s_heads = _dot(q2, kt)
        inter = _dot(qb, st.astype(BF16))
        w_max = jnp.max(w, axis=0, keepdims=True)
        a_s = jnp.exp2(w - w_max)
        upd = jnp.where(bd_mask, _dot(kt, jnp.concatenate([a_s * v, a_s], axis=1).astype(BF16)), 0.0)
        f_tot = fc[n_l - 1:n_l, :] if direction == 0 else fc[0:1, :]
        mw = jnp.maximum(m_pl, w_max)
        a_prev = jnp.exp2(m_pl - mw)
        a_new = jnp.exp2(w_max - mw)
        st_new = jnp.concatenate([a_prev, a_prev], axis=1) * st + jnp.concatenate([a_new, a_new], axis=1) * upd
        early.append((s_heads, inter, m_pl, st_new, f_tot + mw))

    weights = []
    for (direction, qb, q2, kt, v, v_heads, fc, w, b_rows, state), (s_heads, _, _, _, _) in zip(items, early):
        first = slice(0, half) if direction == 0 else slice(half, n_l)
        last = slice(half, n_l) if direction == 0 else slice(0, half)
        p_heads, cm_heads = [], []
        for e in range(2):
            s = s_heads[e * n_l:(e + 1) * n_l]
            bm_a = b_rows[e][:, first] + dmask_ref[direction, first, first]
            bm_b = b_rows[e] + dmask_ref[direction, last, :]
            cm_a = jnp.max(bm_a, axis=-1, keepdims=True)
            cm_b = jnp.max(bm_b, axis=-1, keepdims=True)
            p_a = (s[first, first] * jnp.exp2(bm_a - cm_a)).astype(BF16)
            p_b = (s[last, :] * jnp.exp2(bm_b - cm_b)).astype(BF16)
            if direction == 0:
                p_heads.append(jnp.concatenate([jnp.concatenate([p_a, zeros], axis=1), p_b], axis=0))
                cm_heads.append(jnp.concatenate([cm_a, cm_b], axis=0))
            else:
                p_heads.append(jnp.concatenate([p_b, jnp.concatenate([zeros, p_a], axis=1)], axis=0))
                cm_heads.append(jnp.concatenate([cm_b, cm_a], axis=0))
        weights.append((jnp.concatenate(p_heads, axis=1), jnp.where(head0, cm_heads[0], cm_heads[1])))

    intras = [_dot(p, jnp.concatenate([item[5], ones_blk], axis=1)) for item, (p, _) in zip(items, weights)]

    results = []
    for item, (_, inter, m_pl, st_new, m_new), (_, cm_pl), intra in zip(items, early, weights, intras):
        fc = item[6]
        mx = jnp.maximum(m_pl, cm_pl)
        w_intra = jnp.exp2(cm_pl - mx)
        w_inter = jnp.exp2(m_pl - mx)
        num = w_inter * inter[:, :LANES] + w_intra * intra[:, :LANES]
        den = w_inter * inter[:, LANES:] + w_intra * intra[:, LANES:]
        results.append((num / jnp.maximum(jnp.abs(den), jnp.exp2(-(fc + mx))), st_new, m_new))
    return results


def _mix_c_kernel(zq_ref, zk_ref, zv_ref, zo_ref, rg_ref, cq_ref, ck_ref, cv_ref, co_ref, crg_ref,
                  wq_ref, wk_ref, bq_ref, bk_ref, cos_ref, sin_ref, esel_ref, tri_ref, dmask_ref,
                  o_ref, oc_ref, qs_ref, q2_ref, kt_ref, vh_ref, fw_ref, br_ref):
    s_len = zq_ref.shape[2]
    n_l = C_LEN
    n_chunks = s_len // n_l
    lane = lax.broadcasted_iota(jnp.int32, (n_l, LANES), 1)
    row = lax.broadcasted_iota(jnp.int32, (n_l, LANES), 0)
    k_scale = HEAD_DIM ** -0.5

    def conv_silu(x_ref, t0, total, w_ref_, b_ref_):
        x = x_ref[0, 0, pl.ds(t0, n_l), :]
        if total == n_l:
            prev_row = next_row = jnp.zeros((1, LANES), F32)
        else:
            prev_row = x_ref[0, 0, pl.ds(jnp.maximum(t0 - 1, 0), 1), :] * (t0 > 0).astype(F32)
            next_row = x_ref[0, 0, pl.ds(jnp.minimum(t0 + n_l, total - 1), 1), :] * (t0 + n_l < total).astype(F32)
        x_prev = jnp.where(row == 0, prev_row, pltpu.roll(x, 1, axis=0))
        x_next = jnp.where(row == n_l - 1, next_row, pltpu.roll(x, n_l - 1, axis=0))
        y = x_prev * w_ref_[0, 0:1, :] + x * w_ref_[0, 1:2, :] + x_next * w_ref_[0, 2:3, :] + b_ref_[0]
        return _silu(y)

    def rope(x, t0):
        cos = cos_ref[pl.ds(t0, n_l), :]
        sin = sin_ref[pl.ds(t0, n_l), :]
        first = (lane % (HEAD_DIM // 2)) < (HEAD_DIM // 4)
        partner = jnp.where(first, pltpu.roll(x, LANES - HEAD_DIM // 4, axis=1), pltpu.roll(x, HEAD_DIM // 4, axis=1))
        return x * cos + partner * sin

    def qk_prep(q_in, k_in, t0, total, with_rope):
        q = conv_silu(q_in, t0, total, wq_ref, bq_ref)
        k = conv_silu(k_in, t0, total, wk_ref, bk_ref)
        if with_rope:
            q, k = rope(q, t0), rope(k, t0)
        return q, (k * k_scale).T.astype(BF16)

    def gate_prep(rg, d):
        n_g = rg.shape[0]
        f3 = _dot(_stack3(rg), tri_ref[1 - d])
        f_row = f3[:n_g] + f3[n_g:2 * n_g] + f3[2 * n_g:]
        b_rows = [rg[2 * d + e:2 * d + e + 1, :] - f_row[4 + 2 * d + e:5 + 2 * d + e, :] for e in range(2)]
        fw = lax.dot_general(_stack3(jnp.concatenate([f_row, rg], axis=0)), esel_ref[d], (((0,), (0,)), ((), ())),
                             preferred_element_type=F32)
        return fw[:, :LANES], fw[:, LANES:], b_rows

    head0 = lane < HEAD_DIM

    head_lanes = [jnp.where((lane[0:1, :] < HEAD_DIM) == (e == 0), 1.0, 0.0).astype(BF16) for e in range(2)]

    def per_head(x):
        xb = x.astype(BF16)
        return jnp.concatenate([xb * head_lanes[0], xb * head_lanes[1]], axis=0)

    ones_blk = jnp.concatenate([jnp.where(head0, 1.0, 0.0), jnp.where(head0, 0.0, 1.0)], axis=0).astype(BF16)
    bd_r = lax.broadcasted_iota(jnp.int32, (LANES, 2 * LANES), 0)
    bd_c = lax.broadcasted_iota(jnp.int32, (LANES, 2 * LANES), 1)
    bd_mask = (bd_r < HEAD_DIM) == ((bd_c % LANES) < HEAD_DIM)
    consts = (head0, ones_blk, bd_mask)

    cq, ckt = qk_prep(cq_ref, ck_ref, 0, n_l, False)
    cv = cv_ref[0, 0]
    crg = crg_ref[0, 0] * LOG2E
    items = []
    for d in range(2):
        items.append((d, cq.astype(BF16), per_head(cq), ckt, cv, per_head(cv), *gate_prep(crg, d),
                      (jnp.zeros((LANES, 2 * LANES), F32), jnp.zeros((1, LANES), F32))))
    res = _mlstm_chunks(items, dmask_ref, *consts)
    oc_ref[0, 0] = (res[0][0] + res[1][0]) * _sigmoid(co_ref[0, 0])
    carry = [(st, m_pl) for _, st, m_pl in res]

    def prep(c, carry):
        t0 = pl.multiple_of(c * n_l, n_l)
        rows = pl.ds(t0, n_l)
        q, kt_ref[:, rows] = qk_prep(zq_ref, zk_ref, t0, s_len, True)
        qs_ref[rows, :] = q.astype(BF16)
        q2_ref[:, rows, :] = per_head(q).reshape(2, n_l, LANES)
        vh_ref[:, rows, :] = per_head(zv_ref[0, 0, rows, :]).reshape(2, n_l, LANES)
        rg = rg_ref[0, 0, :, rows] * LOG2E
        for d in range(2):
            fc, w, b_rows = gate_prep(rg, d)
            fw_ref[d, rows, :] = jnp.concatenate([fc, w], axis=1)
            for e in range(2):
                br_ref[2 * d + e:2 * d + e + 1, rows] = b_rows[e]
        return carry

    lax.fori_loop(0, n_chunks, prep, 0, unroll=4)

    def make_step(second_visit):
        def step(i, carry):
            items, all_rows = [], []
            for d in range(2):
                c = i if d == 0 else n_chunks - 1 - i
                rows = pl.ds(pl.multiple_of(c * n_l, n_l), n_l)
                fw = fw_ref[d, rows, :]
                b_rows = [br_ref[2 * d + e:2 * d + e + 1, rows] for e in range(2)]
                items.append((d, qs_ref[rows, :], q2_ref[:, rows, :].reshape(2 * n_l, LANES), kt_ref[:, rows],
                              zv_ref[0, 0, rows, :], vh_ref[:, rows, :].reshape(2 * n_l, LANES),
                              fw[:, :LANES], fw[:, LANES:], b_rows, carry[d]))
                all_rows.append(rows)
            res = _mlstm_chunks(items, dmask_ref, *consts)
            for rows, (h, _, _) in zip(all_rows, res):
                if second_visit:
                    o_ref[0, 0, rows, :] = (o_ref[0, 0, rows, :] + h) * _sigmoid(zo_ref[0, 0, rows, :])
                else:
                    o_ref[0, 0, rows, :] = h
            return tuple((st, m_pl) for _, st, m_pl in res)
        return step

    carry = lax.fori_loop(0, n_chunks // 2, make_step(False), tuple(carry), unroll=2)
    lax.fori_loop(n_chunks // 2, n_chunks, make_step(True), carry, unroll=2)


def _mix_c_call(z, zg, zc, zcg, wconv, bconv, cos_t, sin_t, esel, tri, dmask):
    b, _, s, _ = z.shape
    cl = zc.shape[2]
    assert cl == C_LEN and s % (4 * C_LEN) == 0
    once = dict(pipeline_mode=pl.Buffered(1))
    zt = lambda t: pl.BlockSpec((1, 1, s, LANES), lambda i, p: (i, t + p, 0, 0))
    ct = lambda t: pl.BlockSpec((1, 1, cl, LANES), lambda i, p: (i, t + p, 0, 0))
    return pl.pallas_call(
        _mix_c_kernel,
        grid=(b, N_PAIRS),
        in_specs=[zt(T_CQ), zt(T_CK), zt(T_CV), zt(T_CO),
                  pl.BlockSpec((1, 1, PAIR_GATES, s), lambda i, p: (i, p, 0, 0)),
                  ct(T_CQ), ct(T_CK), ct(T_CV), ct(T_CO),
                  pl.BlockSpec((1, 1, PAIR_GATES, cl), lambda i, p: (i, p, 0, 0)),
                  pl.BlockSpec((1, 3, LANES), lambda i, p: (p, 0, 0)),
                  pl.BlockSpec((1, 3, LANES), lambda i, p: (N_PAIRS + p, 0, 0)),
                  pl.BlockSpec((1, 1, LANES), lambda i, p: (p, 0, 0)),
                  pl.BlockSpec((1, 1, LANES), lambda i, p: (N_PAIRS + p, 0, 0)),
                  pl.BlockSpec((s, LANES), lambda i, p: (0, 0), **once),
                  pl.BlockSpec((s, LANES), lambda i, p: (0, 0), **once),
                  pl.BlockSpec((2, 48, 2 * LANES), lambda i, p: (0, 0, 0), **once),
                  pl.BlockSpec((2, C_LEN, C_LEN), lambda i, p: (0, 0, 0), **once),
                  pl.BlockSpec((2, C_LEN, C_LEN), lambda i, p: (0, 0, 0), **once)],
        out_specs=[pl.BlockSpec((1, 1, s, LANES), lambda i, p: (i, p, 0, 0)),
                   pl.BlockSpec((1, 1, cl, LANES), lambda i, p: (i, p, 0, 0))],
        out_shape=[jax.ShapeDtypeStruct((b, N_PAIRS, s, LANES), F32),
                   jax.ShapeDtypeStruct((b, N_PAIRS, cl, LANES), F32)],
        scratch_shapes=[pltpu.VMEM((s, LANES), BF16), pltpu.VMEM((2, s, LANES), BF16), pltpu.VMEM((LANES, s), BF16),
                        pltpu.VMEM((2, s, LANES), BF16), pltpu.VMEM((2, s, 2 * LANES), F32), pltpu.VMEM((PAIR_GATES, s), F32)],
        compiler_params=_cparams(("arbitrary", "arbitrary")),
        name="mix_c_mlstm",
    )(z, z, z, z, zg.reshape(b, N_PAIRS, PAIR_GATES, s), zc, zc, zc, zc, zcg.reshape(b, N_PAIRS, PAIR_GATES, cl),
      wconv, wconv, bconv, bconv, cos_t, sin_t, esel, tri, dmask)


def _rope_tables(s):
    t = np.arange(s)
    quarter = HEAD_DIM // 4
    inv_freq = (ROPE_BASE ** (-np.arange(quarter, dtype=np.float32) / quarter)).astype(np.float32)
    lane = np.arange(LANES) % HEAD_DIM
    use_col = lane >= HEAD_DIM // 2
    second = (lane % (HEAD_DIM // 2)) >= quarter
    pos = np.where(use_col[None, :], (t % GRID_W)[:, None], (t // GRID_W)[:, None]).astype(np.float32)
    ang = (pos * inv_freq[lane % quarter][None, :]).astype(np.float64)
    sin = np.where(second[None, :], np.sin(ang), -np.sin(ang))
    return jnp.asarray(np.cos(ang), F32), jnp.asarray(sin, F32)


def _gate_tables():
    esel = np.zeros((2, 48, 2 * LANES), np.float32)
    for d in range(2):
        for ln in range(LANES):
            e = ln // HEAD_DIM
            esel[d, 4 + 2 * d + e, ln] = 1.0
            esel[d, 4 + 2 * d + e, LANES + ln] = -1.0
            esel[d, 8 + 2 * d + e, LANES + ln] = 1.0
    esel[:, 16:32, :] = esel[:, :16, :]
    esel[:, 32:, :] = esel[:, :16, :]
    t = np.arange(C_LEN)
    tri = np.stack([(t[None, :] <= t[:, None]), (t[None, :] >= t[:, None])])
    dmask = np.where(tri, 0.0, NEG).astype(np.float32)
    return jnp.asarray(esel, BF16), jnp.asarray(tri.astype(np.float32), BF16), jnp.asarray(dmask)


def _post_kernel(x_ref, ya_ref, yb_ref, yc_ref, mod_ref, wout_ref, g_ref, wg_ref, wu_ref, wd_ref, fg_ref, o_ref,
                 *, final, ff_chunk):
    y = jnp.concatenate([ya_ref[0, 0], ya_ref[0, 1], yb_ref[0, 0], yb_ref[0, 1], yb_ref[0, 2],
                         yc_ref[0, 0], yc_ref[0, 1], yc_ref[0, 2]], axis=1).astype(BF16)
    x1 = x_ref[0] + mod_ref[0, 2:3, :] * _dot(y, wout_ref[...])
    hm = _rms_modulate(x1, g_ref[...], mod_ref[0, 3:4, :], mod_ref[0, 4:5, :]).astype(BF16)
    acc = jnp.zeros_like(x1)
    for c in range(wg_ref.shape[1] // ff_chunk):
        cols = slice(c * ff_chunk, (c + 1) * ff_chunk)
        a = _silu(_dot(hm, wg_ref[:, cols])) * _dot(hm, wu_ref[:, cols])
        acc = acc + _dot(a.astype(BF16), wd_ref[cols, :])
    x2 = x1 + mod_ref[0, 5:6, :] * acc
    if final:
        ms = jnp.mean(x2 * x2, axis=-1, keepdims=True)
        x2 = x2 * lax.rsqrt(ms + EPS) * fg_ref[...]
    o_ref[0] = x2


def _post_call(x, ya, yb, yc, mod, wout, g, wg, wu, wd, fg, tm, final):
    b, s, d = x.shape
    dff = wg.shape[1]
    assert dff % MXU_N == 0
    per_sample = mod.shape[0] == b
    mod_map = (lambda i, j: (i, 0, 0)) if per_sample else (lambda i, j: (0, 0, 0))
    const = lambda i, j: (0, 0)
    once = dict(pipeline_mode=pl.Buffered(1))
    yt = lambda n: pl.BlockSpec((1, n, tm, LANES), lambda i, j: (i, 0, j, 0))
    return pl.pallas_call(
        functools.partial(_post_kernel, final=final, ff_chunk=MXU_N),
        grid=(b, s // tm),
        in_specs=[pl.BlockSpec((1, tm, d), lambda i, j: (i, j, 0)),
                  yt(2), yt(N_PAIRS), yt(N_PAIRS),
                  pl.BlockSpec((1, N_MOD, d), mod_map),
                  pl.BlockSpec((d, d), const, **once),
                  pl.BlockSpec((1, d), const),
                  pl.BlockSpec((d, dff), const, **once),
                  pl.BlockSpec((d, dff), const, **once),
                  pl.BlockSpec((dff, d), const, **once),
                  pl.BlockSpec((1, d), const)],
        out_specs=pl.BlockSpec((1, tm, d), lambda i, j: (i, j, 0)),
        out_shape=jax.ShapeDtypeStruct((b, s, d), F32),
        compiler_params=_cparams(("arbitrary", "arbitrary")),
        name="out_proj_swiglu",
    )(x, ya, yb, yc, mod, wout, g, wg, wu, wd, fg)


def _gate_perm():
    row = np.zeros(N_GATES, np.int32)
    for p in range(N_PAIRS):
        for k in range(4):
            for e in range(2):
                row[PAIR_GATES * p + 4 * (k % 2) + 2 * (k // 2) + e] = C_HEADS * k + 2 * p + e
    return row


def kernel(x, c, ctx, c_ctx, w_mod, b_mod, norm1_g, w_in, a_ln_g, a_ln_b, a_ws, a_bs, b_rpb,
           c_conv_w, c_conv_b, c_gate_b, w_out, norm2_g, w_gate, w_up, w_down, final_g):
    b, s, d = x.shape
    cl = ctx.shape[1]
    depth = w_mod.shape[0]
    n_main = (A_TILES + N_TILES) * LANES
    tm_proj, tm = min(1024, s), min(512, s)
    perm = _gate_perm()

    pad = (-(b + 1)) % 8
    cs = jnp.concatenate([c, c_ctx[None, :], jnp.zeros((pad, d), F32)], axis=0)
    mod_all = _mod_call(cs, w_mod.astype(BF16), b_mod)

    cos_t, sin_t = _rope_tables(s)
    esel, tri, dmask = _gate_tables()
    bias_all = _natten_bias(b_rpb.reshape((-1,) + b_rpb.shape[2:]) * LOG2E, s // GRID_W)
    avg = np.kron(np.eye(2), np.full((HEAD_DIM, HEAD_DIM), 1.0 / HEAD_DIM))
    avg = jnp.asarray(np.concatenate([avg, avg], axis=0), BF16)

    xc = ctx
    for l in range(depth):
        last = l == depth - 1
        mod = mod_all[l, :b].reshape(b, N_MOD, d)
        mod_c = mod_all[l, b:b + 1].reshape(1, N_MOD, d)

        w_main = jnp.concatenate([w_in[l][:, :n_main], w_in[l][:, n_main:][:, perm],
                                  jnp.zeros((d, LANES - N_GATES), F32)], axis=1).astype(BF16)
        gbt = jnp.broadcast_to(c_gate_b[l].reshape(-1)[perm][:, None], (N_GATES, tm_proj))
        g1 = norm1_g[l][None, :]

        mix_a = (a_ln_g[l].reshape(2, 1, LANES), a_ln_b[l].reshape(2, 1, LANES), a_ws[l].astype(BF16),
                 jnp.repeat(a_bs[l].reshape(2, 2, A_CHUNK).transpose(0, 2, 1), HEAD_DIM, axis=2), avg)
        z, zg, ya = _inproj_call(x, mod, g1, w_main, gbt, *mix_a, tm_proj)
        zc, zcg, yac = _inproj_call(xc, mod_c, g1, w_main, gbt, *mix_a, cl)

        yb = _mix_b_call(z, zc, bias_all, l)

        wconv = c_conv_w[l].reshape(3, 2 * N_PAIRS, LANES).transpose(1, 0, 2)
        bconv = c_conv_b[l].reshape(2 * N_PAIRS, 1, LANES)
        yc, ycc = _mix_c_call(z, zg, zc, zcg, wconv, bconv, cos_t, sin_t, esel, tri, dmask)

        weights = (w_out[l].astype(BF16), norm2_g[l][None, :], w_gate[l].astype(BF16), w_up[l].astype(BF16),
                   w_down[l].astype(BF16), final_g[None, :])
        x = _post_call(x, ya, yb, yc, mod, *weights, tm, last)
        if not last:
            ybc = _mix_b_ctx_call(zc)
            xc = _post_call(xc, yac, ybc, ycc, mod_c, *weights, cl, False)
    return x
```

```python
import functools

import numpy as np
import jax
import jax.numpy as jnp
from jax import lax
from jax.experimental import pallas as pl
from jax.experimental.pallas import tpu as pltpu

F32 = jnp.float32
BF16 = jnp.bfloat16

LANES = 128
MXU_N = 256
HEAD_DIM = 64
GRID_W = 64
A_GROUPS = 4
A_CHUNK = 128
B_HEADS = 6
C_HEADS = 6
WIN_ROWS = 8
WIN_COLS = 16
ROPE_BASE = 10000.0
EPS = 1e-6
N_MOD = 6
NEG = -1e30
LOG2E = float(np.log2(np.e))

A_TILES = 4
T_BQ, T_BK, T_BV, T_CQ, T_CK, T_CV, T_CO = 0, 3, 6, 9, 12, 15, 18
N_TILES = 21
N_PAIRS = 3
N_GATES = 4 * C_HEADS
PAIR_GATES = N_GATES // N_PAIRS
Q_ROWS = 4
K_ROWS = Q_ROWS + WIN_ROWS
C_LEN = 256

VMEM_LIMIT = 56 * 1024 * 1024


def _cparams(sem):
    return pltpu.CompilerParams(dimension_semantics=sem, vmem_limit_bytes=VMEM_LIMIT)


def _dot(a, b):
    return jnp.dot(a, b, preferred_element_type=F32)


def _dot_nt(a, b):
    return lax.dot_general(a, b, (((1,), (1,)), ((), ())), preferred_element_type=F32)


def _split2(x):
    h = x.astype(BF16)
    return h, (x - h.astype(F32)).astype(BF16)


def _dot2_r(x, sel2):
    h, l = _split2(x)
    return _dot(jnp.concatenate([h, l], axis=1), sel2)


def _stack3(x):
    h = x.astype(BF16).astype(F32)
    r = x - h
    m = r.astype(BF16).astype(F32)
    return jnp.concatenate([h, m, r - m], axis=0).astype(BF16)


def _silu(x):
    h = 0.5 * x
    return h + h * jnp.tanh(h)


def _sigmoid(x):
    return 0.5 + 0.5 * jnp.tanh(0.5 * x)


def _gelu_tanh(x):
    return 0.5 * x * (1.0 + jnp.tanh(np.sqrt(2.0 / np.pi).astype(np.float32) * (x + 0.044715 * (x * x * x))))


def _log_sigmoid(x):
    return jnp.minimum(x, 0.0) - jnp.log1p(jnp.exp(-jnp.abs(x)))


def _rms_modulate(x, g, shift, scale):
    ms = jnp.mean(x * x, axis=-1, keepdims=True)
    return (x * lax.rsqrt(ms + EPS) * g) * (1.0 + scale) + shift


def _mod_kernel(c_ref, w_ref, b_ref, o_ref):
    o_ref[0] = _dot(_silu(c_ref[...]).astype(BF16), w_ref[0]) + b_ref[0, 0]


def _mod_call(cs, w_mod, b_mod):
    depth, d, nd = w_mod.shape
    rows = cs.shape[0]
    return pl.pallas_call(
        _mod_kernel,
        grid=(depth, nd // d),
        in_specs=[pl.BlockSpec((rows, d), lambda l, n: (0, 0)),
                  pl.BlockSpec((1, d, d), lambda l, n: (l, 0, n)),
                  pl.BlockSpec((1, 1, 1, d), lambda l, n: (l, n, 0, 0))],
        out_specs=pl.BlockSpec((1, rows, d), lambda l, n: (l, 0, n)),
        out_shape=jax.ShapeDtypeStruct((depth, rows, nd), F32),
        compiler_params=_cparams(("arbitrary", "arbitrary")),
        name="adaln_mod",
    )(cs, w_mod, b_mod.reshape(depth, nd // d, 1, d))


def _gmlp(u, v, j, lng_ref, lnb_ref, ws_ref, bsx_ref, avg):
    lane = lax.broadcasted_iota(jnp.int32, (A_CHUNK, LANES), 1)
    v = _gelu_tanh(v)
    dv = v - _dot2_r(v, avg)
    var = _dot2_r(dv * dv, avg)
    vn = (dv * lax.rsqrt(var + EPS) * lng_ref[j] + lnb_ref[j]).astype(BF16)
    u = _gelu_tanh(u)
    outs = []
    for c in range(u.shape[0] // A_CHUNK):
        rows = slice(c * A_CHUNK, (c + 1) * A_CHUNK)
        s = jnp.where(lane < HEAD_DIM, _dot(ws_ref[2 * j], vn[rows]), _dot(ws_ref[2 * j + 1], vn[rows])) + bsx_ref[j]
        outs.append(u[rows] * s)
    return jnp.concatenate(outs, axis=0)


def _inproj_kernel(x_ref, mod_ref, g_ref, w_ref, gbt_ref, lng_ref, lnb_ref, ws_ref, bsx_ref, avg_ref,
                   z_ref, zgt_ref, ya_ref):
    xm = _rms_modulate(x_ref[0], g_ref[...], mod_ref[0, 0:1, :], mod_ref[0, 1:2, :])
    xb = xm.astype(BF16)
    u = _dot(xb, w_ref[:, :MXU_N])
    v = _dot(xb, w_ref[:, MXU_N:2 * MXU_N])
    for j in range((N_TILES + 1) // 2):
        r = _dot(xb, w_ref[:, (A_TILES // 2 + j) * MXU_N:(A_TILES // 2 + j + 1) * MXU_N])
        z_ref[0, 2 * j] = r[:, :LANES]
        if 2 * j + 1 < N_TILES:
            z_ref[0, 2 * j + 1] = r[:, LANES:]
    g = r[:, LANES:].T[:N_GATES, :] + gbt_ref[...]
    sub = lax.broadcasted_iota(jnp.int32, g.shape, 0)
    zgt_ref[0] = jnp.where(sub % PAIR_GATES >= PAIR_GATES // 2, _log_sigmoid(g), g)
    for j in range(2):
        cols = slice(j * LANES, (j + 1) * LANES)
        ya_ref[0, j] = _gmlp(u[:, cols], v[:, cols], j, lng_ref, lnb_ref, ws_ref, bsx_ref, avg_ref[...])


def _inproj_call(x, mod, g, w, gbt, lng, lnb, ws, bsx, avg, tm):
    b, s, d = x.shape
    per_sample = mod.shape[0] == b
    mod_map = (lambda i, j: (i, 0, 0)) if per_sample else (lambda i, j: (0, 0, 0))
    const = lambda i, j: (0, 0)
    c3 = lambda i, j: (0, 0, 0)
    return pl.pallas_call(
        _inproj_kernel,
        grid=(b, s // tm),
        in_specs=[pl.BlockSpec((1, tm, d), lambda i, j: (i, j, 0)),
                  pl.BlockSpec((1, N_MOD, d), mod_map),
                  pl.BlockSpec((1, d), const),
                  pl.BlockSpec((d, (A_TILES + N_TILES + 1) * LANES), const, pipeline_mode=pl.Buffered(1)),
                  pl.BlockSpec((N_GATES, tm), const),
                  pl.BlockSpec((2, 1, LANES), c3),
                  pl.BlockSpec((2, 1, LANES), c3),
                  pl.BlockSpec((A_GROUPS, A_CHUNK, A_CHUNK), c3),
                  pl.BlockSpec((2, A_CHUNK, LANES), c3),
                  pl.BlockSpec((2 * LANES, LANES), const)],
        out_specs=[pl.BlockSpec((1, N_TILES, tm, LANES), lambda i, j: (i, 0, j, 0)),
                   pl.BlockSpec((1, N_GATES, tm), lambda i, j: (i, 0, j)),
                   pl.BlockSpec((1, 2, tm, LANES), lambda i, j: (i, 0, j, 0))],
        out_shape=[jax.ShapeDtypeStruct((b, N_TILES, s, LANES), F32),
                   jax.ShapeDtypeStruct((b, N_GATES, s), F32),
                   jax.ShapeDtypeStruct((b, 2, s, LANES), F32)],
        compiler_params=_cparams(("arbitrary", "arbitrary")),
        name="in_proj",
    )(x, mod, g, w, gbt[:, :tm], lng, lnb, ws, bsx, avg)


def _mix_b_kernel(q_ref, k_ref, v_ref, kc_ref, vc_ref, bias_ref, o_ref, kb_ref, vb_ref, *, n_rows):
    n_blocks = n_rows // Q_ROWS
    nq = Q_ROWS * GRID_W
    nk = K_ROWS * GRID_W
    kb_ref[...] = k_ref[0, 0].astype(BF16)
    vb_ref[...] = v_ref[0, 0].astype(BF16)
    kc = kc_ref[0, 0].astype(BF16)
    vc = vc_ref[0, 0].astype(BF16)
    lane = lax.broadcasted_iota(jnp.int32, (nq, LANES), 1)

    def block_pair(i, carry):
        units = []
        for rb in (2 * i, 2 * i + 1):
            q_start = pl.multiple_of(rb * nq, nq)
            k_start = pl.multiple_of(jnp.clip(rb * Q_ROWS - WIN_ROWS // 2, 0, n_rows - K_ROWS) * GRID_W, nq)
            cls = jnp.where(rb == 0, 0, jnp.where(rb == n_blocks - 1, 2, 1))
            q = q_ref[0, 0, pl.ds(q_start, nq), :] * (HEAD_DIM ** -0.5 * LOG2E)
            kw = kb_ref[pl.ds(k_start, nk), :]
            for e in range(2):
                qe = jnp.where((lane < HEAD_DIM) == (e == 0), q, 0.0).astype(BF16)
                units.append((q_start, k_start, _dot_nt(qe, kw) + bias_ref[0, e, cls], _dot_nt(qe, kc)))
        probs = []
        for _, _, s_w, s_c in units:
            m = jnp.maximum(jnp.max(s_w, axis=-1, keepdims=True), jnp.max(s_c, axis=-1, keepdims=True))
            p_w = jnp.exp2(s_w - m)
            p_c = jnp.exp2(s_c - m)
            den = jnp.sum(p_w, axis=-1, keepdims=True) + jnp.sum(p_c, axis=-1, keepdims=True)
            probs.append((p_w.astype(BF16), p_c.astype(BF16), den))
        outs = [(_dot(p_w, vb_ref[pl.ds(k_start, nk), :]) + _dot(p_c, vc)) / den
                for (_, k_start, _, _), (p_w, p_c, den) in zip(units, probs)]
        for b2 in range(2):
            o_ref[0, 0, pl.ds(units[2 * b2][0], nq), :] = jnp.where(lane < HEAD_DIM, outs[2 * b2], outs[2 * b2 + 1])
        return carry

    lax.fori_loop(0, n_blocks // 2, block_pair, 0)


def _mix_b_call(z, zc, bias, layer):
    b, _, s, _ = z.shape
    cl = zc.shape[2] // b
    n_rows = s // GRID_W
    assert n_rows % (2 * Q_ROWS) == 0 and n_rows >= K_ROWS
    zt = lambda t: pl.BlockSpec((1, 1, s, LANES), lambda p, i: (i, t + p, 0, 0))
    ct = lambda t: pl.BlockSpec((1, 1, cl, LANES), lambda p, i: (0, t + p, i, 0))
    return pl.pallas_call(
        functools.partial(_mix_b_kernel, n_rows=n_rows),
        grid=(N_PAIRS, b),
        in_specs=[zt(T_BQ), zt(T_BK), zt(T_BV), ct(T_BK), ct(T_BV),
                  pl.BlockSpec((1, 2, 3, Q_ROWS * GRID_W, K_ROWS * GRID_W), lambda p, i: (layer * N_PAIRS + p, 0, 0, 0, 0))],
        out_specs=pl.BlockSpec((1, 1, s, LANES), lambda p, i: (i, p, 0, 0)),
        out_shape=jax.ShapeDtypeStruct((b, N_PAIRS, s, LANES), F32),
        scratch_shapes=[pltpu.VMEM((s, LANES), BF16), pltpu.VMEM((s, LANES), BF16)],
        compiler_params=_cparams(("arbitrary", "arbitrary")),
        name="mix_b_natten",
    )(z, z, z, zc, zc, bias)


def _mix_b_ctx_kernel(q_ref, k_ref, v_ref, o_ref):
    q = q_ref[0, 0] * (HEAD_DIM ** -0.5)
    k = k_ref[0, 0].astype(BF16)
    v = v_ref[0, 0].astype(BF16)
    lane = lax.broadcasted_iota(jnp.int32, q.shape, 1)
    outs = []
    for e in range(2):
        qe = jnp.where((lane < HEAD_DIM) == (e == 0), q, 0.0).astype(BF16)
        s = _dot_nt(qe, k)
        p = jnp.exp(s - jnp.max(s, axis=-1, keepdims=True))
        outs.append(_dot(p.astype(BF16), v) / jnp.sum(p, axis=-1, keepdims=True))
    o_ref[0, 0] = jnp.where(lane < HEAD_DIM, outs[0], outs[1])


def _mix_b_ctx_call(zc, b):
    cl = zc.shape[2] // b
    blk = lambda t: pl.BlockSpec((1, 1, cl, LANES), lambda i, p: (0, t + p, i, 0))
    return pl.pallas_call(
        _mix_b_ctx_kernel,
        grid=(b, N_PAIRS),
        in_specs=[blk(T_BQ), blk(T_BK), blk(T_BV)],
        out_specs=blk(0),
        out_shape=jax.ShapeDtypeStruct((1, N_PAIRS, b * cl, LANES), F32),
        compiler_params=_cparams(("arbitrary", "arbitrary")),
        name="mix_b_ctx",
    )(zc, zc, zc)


def _natten_bias(rpb, n_rows):
    h = rpb.shape[0]
    padded = jnp.pad(rpb, ((0, 0), (0, 0), (GRID_W, GRID_W)))
    toe = jnp.stack([padded[:, :, GRID_W + WIN_COLS - 1 - c:2 * GRID_W + WIN_COLS - 1 - c] for c in range(GRID_W)], axis=2)
    c = np.arange(GRID_W)[:, None]
    kc = np.arange(GRID_W)[None, :]
    cs = np.clip(c - WIN_COLS // 2, 0, GRID_W - WIN_COLS)
    col_ok = (kc >= cs) & (kc < cs + WIN_COLS)
    toe = jnp.where(col_ok[None, None], toe, NEG)
    neg_block = jnp.full((h, GRID_W, GRID_W), NEG, F32)
    classes = []
    for r0 in (0, Q_ROWS, n_rows - Q_ROWS):
        k0 = int(np.clip(r0 - WIN_ROWS // 2, 0, n_rows - K_ROWS))
        q_rows = []
        for i in range(Q_ROWS):
            r = r0 + i
            rs = int(np.clip(r - WIN_ROWS // 2, 0, n_rows - WIN_ROWS))
            blocks = []
            for j in range(K_ROWS):
                kr = k0 + j
                blocks.append(toe[:, kr - r + WIN_ROWS - 1] if rs <= kr < rs + WIN_ROWS else neg_block)
            q_rows.append(jnp.concatenate(blocks, axis=-1))
        classes.append(jnp.concatenate(q_rows, axis=1))
    bias = jnp.stack(classes, axis=1)
    return bias.reshape(h // 2, 2, 3, Q_ROWS * GRID_W, K_ROWS * GRID_W)


def _mlstm_chunks(items, dmask_ref, head0, ones_blk, bd_mask):
    n_l = items[0][1].shape[0]
    half = n_l // 2
    zeros = jnp.zeros((half, half), BF16)

    early = []
    for (direction, qb, q2, kt, v, v_heads, fc, w, b_rows, state) in items:
        st, m_pl = state
        s_heads = _dot(q2, kt)
        inter = _dot(qb, st.astype(BF16))
        w_max = jnp.max(w, axis=0, keepdims=True)
        a_s = jnp.exp2(w - w_max)
        upd = jnp.where(bd_mask, _dot(kt, jnp.concatenate([a_s * v, a_s], axis=1).astype(BF16)), 0.0)
        f_tot = fc[n_l - 1:n_l, :] if direction == 0 else fc[0:1, :]
        mw = jnp.maximum(m_pl, w_max)
        a_prev = jnp.exp2(m_pl - mw)
        a_new = jnp.exp2(w_max - mw)
        st_new = jnp.concatenate([a_prev, a_prev], axis=1) * st + jnp.concatenate([a_new, a_new], axis=1) * upd
        early.append((s_heads, inter, m_pl, st_new, f_tot + mw))

    weights = []
    for (direction, qb, q2, kt, v, v_heads, fc, w, b_rows, state), (s_heads, _, _, _, _) in zip(items, early):
        first = slice(0, half) if direction == 0 else slice(half, n_l)
        last = slice(half, n_l) if direction == 0 else slice(0, half)
        p_heads, cm_heads = [], []
        for e in range(2):
            s = s_heads[e * n_l:(e + 1) * n_l]
            bm_a = b_rows[e][:, first] + dmask_ref[direction, first, first]
            bm_b = b_rows[e] + dmask_ref[direction, last, :]
            cm_a = jnp.max(bm_a, axis=-1, keepdims=True)
            cm_b = jnp.max(bm_b, axis=-1, keepdims=True)
            p_a = (s[first, first] * jnp.exp2(bm_a - cm_a)).astype(BF16)
            p_b = (s[last, :] * jnp.exp2(bm_b - cm_b)).astype(BF16)
            if direction == 0:
                p_heads.append(jnp.concatenate([jnp.concatenate([p_a, zeros], axis=1), p_b], axis=0))
                cm_heads.append(jnp.concatenate([cm_a, cm_b], axis=0))
            else:
                p_heads.append(jnp.concatenate([p_b, jnp.concatenate([zeros, p_a], axis=1)], axis=0))
                cm_heads.append(jnp.concatenate([cm_b, cm_a], axis=0))
        weights.append((jnp.concatenate(p_heads, axis=1), jnp.where(head0, cm_heads[0], cm_heads[1])))

    intras = [_dot(p, jnp.concatenate([item[5], ones_blk], axis=1)) for item, (p, _) in zip(items, weights)]

    results = []
    for item, (_, inter, m_pl, st_new, m_new), (_, cm_pl), intra in zip(items, early, weights, intras):
        fc = item[6]
        mx = jnp.maximum(m_pl, cm_pl)
        w_intra = jnp.exp2(cm_pl - mx)
        w_inter = jnp.exp2(m_pl - mx)
        num = w_inter * inter[:, :LANES] + w_intra * intra[:, :LANES]
        den = w_inter * inter[:, LANES:] + w_intra * intra[:, LANES:]
        results.append((num / jnp.maximum(jnp.abs(den), jnp.exp2(-(fc + mx))), st_new, m_new))
    return results


def _mix_c_kernel(zq_ref, zk_ref, zv_ref, zo_ref, rg_ref, cq_ref, ck_ref, cv_ref, co_ref, crg_ref,
                  wq_ref, wk_ref, bq_ref, bk_ref, cos_ref, sin_ref, esel_ref, tri_ref, dmask_ref,
                  o_ref, oc_ref, qs_ref, q2_ref, kt_ref, vh_ref, fw_ref, br_ref):
    s_len = zq_ref.shape[2]
    n_l = C_LEN
    n_chunks = s_len // n_l
    lane = lax.broadcasted_iota(jnp.int32, (n_l, LANES), 1)
    row = lax.broadcasted_iota(jnp.int32, (n_l, LANES), 0)
    k_scale = HEAD_DIM ** -0.5

    def conv_silu(x_ref, t0, total, w_ref_, b_ref_):
        x = x_ref[0, 0, pl.ds(t0, n_l), :]
        if total == n_l:
            prev_row = next_row = jnp.zeros((1, LANES), F32)
        else:
            prev_row = x_ref[0, 0, pl.ds(jnp.maximum(t0 - 1, 0), 1), :] * (t0 > 0).astype(F32)
            next_row = x_ref[0, 0, pl.ds(jnp.minimum(t0 + n_l, total - 1), 1), :] * (t0 + n_l < total).astype(F32)
        x_prev = jnp.where(row == 0, prev_row, pltpu.roll(x, 1, axis=0))
        x_next = jnp.where(row == n_l - 1, next_row, pltpu.roll(x, n_l - 1, axis=0))
        y = x_prev * w_ref_[0, 0:1, :] + x * w_ref_[0, 1:2, :] + x_next * w_ref_[0, 2:3, :] + b_ref_[0]
        return _silu(y)

    def rope(x, t0):
        cos = cos_ref[pl.ds(t0, n_l), :]
        sin = sin_ref[pl.ds(t0, n_l), :]
        first = (lane % (HEAD_DIM // 2)) < (HEAD_DIM // 4)
        partner = jnp.where(first, pltpu.roll(x, LANES - HEAD_DIM // 4, axis=1), pltpu.roll(x, HEAD_DIM // 4, axis=1))
        return x * cos + partner * sin

    def qk_prep(q_in, k_in, t0, total, with_rope):
        q = conv_silu(q_in, t0, total, wq_ref, bq_ref)
        k = conv_silu(k_in, t0, total, wk_ref, bk_ref)
        if with_rope:
            q, k = rope(q, t0), rope(k, t0)
        return q, (k * k_scale).T.astype(BF16)

    def gate_prep(rg, d):
        n_g = rg.shape[0]
        f3 = _dot(_stack3(rg), tri_ref[1 - d])
        f_row = f3[:n_g] + f3[n_g:2 * n_g] + f3[2 * n_g:]
        b_rows = [rg[2 * d + e:2 * d + e + 1, :] - f_row[4 + 2 * d + e:5 + 2 * d + e, :] for e in range(2)]
        fw = lax.dot_general(_stack3(jnp.concatenate([f_row, rg], axis=0)), esel_ref[d], (((0,), (0,)), ((), ())),
                             preferred_element_type=F32)
        return fw[:, :LANES], fw[:, LANES:], b_rows

    head0 = lane < HEAD_DIM

    head_lanes = [jnp.where((lane[0:1, :] < HEAD_DIM) == (e == 0), 1.0, 0.0).astype(BF16) for e in range(2)]

    def per_head(x):
        xb = x.astype(BF16)
        return jnp.concatenate([xb * head_lanes[0], xb * head_lanes[1]], axis=0)

    ones_blk = jnp.concatenate([jnp.where(head0, 1.0, 0.0), jnp.where(head0, 0.0, 1.0)], axis=0).astype(BF16)
    bd_r = lax.broadcasted_iota(jnp.int32, (LANES, 2 * LANES), 0)
    bd_c = lax.broadcasted_iota(jnp.int32, (LANES, 2 * LANES), 1)
    bd_mask = (bd_r < HEAD_DIM) == ((bd_c % LANES) < HEAD_DIM)
    consts = (head0, ones_blk, bd_mask)

    cq, ckt = qk_prep(cq_ref, ck_ref, 0, n_l, False)
    cv = cv_ref[0, 0]
    crg = crg_ref[0, 0] * LOG2E
    items = []
    for d in range(2):
        items.append((d, cq.astype(BF16), per_head(cq), ckt, cv, per_head(cv), *gate_prep(crg, d),
                      (jnp.zeros((LANES, 2 * LANES), F32), jnp.zeros((1, LANES), F32))))
    res = _mlstm_chunks(items, dmask_ref, *consts)
    oc_ref[0, 0] = (res[0][0] + res[1][0]) * _sigmoid(co_ref[0, 0])
    carry = [(st, m_pl) for _, st, m_pl in res]

    def prep(c, carry):
        t0 = pl.multiple_of(c * n_l, n_l)
        rows = pl.ds(t0, n_l)
        q, kt_ref[:, rows] = qk_prep(zq_ref, zk_ref, t0, s_len, True)
        qs_ref[rows, :] = q.astype(BF16)
        q2_ref[:, rows, :] = per_head(q).reshape(2, n_l, LANES)
        vh_ref[:, rows, :] = per_head(zv_ref[0, 0, rows, :]).reshape(2, n_l, LANES)
        rg = rg_ref[0, 0, :, rows] * LOG2E
        for d in range(2):
            fc, w, b_rows = gate_prep(rg, d)
            fw_ref[d, rows, :] = jnp.concatenate([fc, w], axis=1)
            for e in range(2):
                br_ref[2 * d + e:2 * d + e + 1, rows] = b_rows[e]
        br_ref[PAIR_GATES // 2:, rows] = jnp.zeros((PAIR_GATES // 2, n_l), F32)
        return carry

    lax.fori_loop(0, n_chunks, prep, 0, unroll=4)

    def make_step(second_visit):
        def step(i, carry):
            items, all_rows = [], []
            for d in range(2):
                c = i if d == 0 else n_chunks - 1 - i
                rows = pl.ds(pl.multiple_of(c * n_l, n_l), n_l)
                fw = fw_ref[d, rows, :]
                br = br_ref[:, rows]
                b_rows = [br[2 * d + e:2 * d + e + 1, :] for e in range(2)]
                items.append((d, qs_ref[rows, :], q2_ref[:, rows, :].reshape(2 * n_l, LANES), kt_ref[:, rows],
                              zv_ref[0, 0, rows, :], vh_ref[:, rows, :].reshape(2 * n_l, LANES),
                              fw[:, :LANES], fw[:, LANES:], b_rows, carry[d]))
                all_rows.append(rows)
            res = _mlstm_chunks(items, dmask_ref, *consts)
            for rows, (h, _, _) in zip(all_rows, res):
                if second_visit:
                    o_ref[0, 0, rows, :] = (o_ref[0, 0, rows, :] + h) * _sigmoid(zo_ref[0, 0, rows, :])
                else:
                    o_ref[0, 0, rows, :] = h
            return tuple((st, m_pl) for _, st, m_pl in res)
        return step

    carry = lax.fori_loop(0, n_chunks // 2, make_step(False), tuple(carry), unroll=2)
    lax.fori_loop(n_chunks // 2, n_chunks, make_step(True), carry, unroll=2)


def _mix_c_call(z, zg, zc, zcg, wconv, bconv, cos_t, sin_t, esel, tri, dmask):
    b, _, s, _ = z.shape
    cl = zc.shape[2] // b
    assert cl == C_LEN and s % (4 * C_LEN) == 0
    once = dict(pipeline_mode=pl.Buffered(1))
    zt = lambda t: pl.BlockSpec((1, 1, s, LANES), lambda i, p: (i, t + p, 0, 0))
    ct = lambda t: pl.BlockSpec((1, 1, cl, LANES), lambda i, p: (0, t + p, i, 0))
    return pl.pallas_call(
        _mix_c_kernel,
        grid=(b, N_PAIRS),
        in_specs=[zt(T_CQ), zt(T_CK), zt(T_CV), zt(T_CO),
                  pl.BlockSpec((1, 1, PAIR_GATES, s), lambda i, p: (i, p, 0, 0)),
                  ct(T_CQ), ct(T_CK), ct(T_CV), ct(T_CO),
                  pl.BlockSpec((1, 1, PAIR_GATES, cl), lambda i, p: (0, p, 0, i)),
                  pl.BlockSpec((1, 3, LANES), lambda i, p: (p, 0, 0)),
                  pl.BlockSpec((1, 3, LANES), lambda i, p: (N_PAIRS + p, 0, 0)),
                  pl.BlockSpec((1, 1, LANES), lambda i, p: (p, 0, 0)),
                  pl.BlockSpec((1, 1, LANES), lambda i, p: (N_PAIRS + p, 0, 0)),
                  pl.BlockSpec((s, LANES), lambda i, p: (0, 0), **once),
                  pl.BlockSpec((s, LANES), lambda i, p: (0, 0), **once),
                  pl.BlockSpec((2, 48, 2 * LANES), lambda i, p: (0, 0, 0), **once),
                  pl.BlockSpec((2, C_LEN, C_LEN), lambda i, p: (0, 0, 0), **once),
                  pl.BlockSpec((2, C_LEN, C_LEN), lambda i, p: (0, 0, 0), **once)],
        out_specs=[pl.BlockSpec((1, 1, s, LANES), lambda i, p: (i, p, 0, 0)),
                   pl.BlockSpec((1, 1, cl, LANES), lambda i, p: (0, p, i, 0))],
        out_shape=[jax.ShapeDtypeStruct((b, N_PAIRS, s, LANES), F32),
                   jax.ShapeDtypeStruct((1, N_PAIRS, b * cl, LANES), F32)],
        scratch_shapes=[pltpu.VMEM((s, LANES), BF16), pltpu.VMEM((2, s, LANES), BF16), pltpu.VMEM((LANES, s), BF16),
                        pltpu.VMEM((2, s, LANES), BF16), pltpu.VMEM((2, s, 2 * LANES), F32), pltpu.VMEM((PAIR_GATES, s), F32)],
        compiler_params=_cparams(("arbitrary", "arbitrary")),
        name="mix_c_mlstm",
    )(z, z, z, z, zg.reshape(b, N_PAIRS, PAIR_GATES, s), zc, zc, zc, zc, zcg.reshape(1, N_PAIRS, PAIR_GATES, b * cl),
      wconv, wconv, bconv, bconv, cos_t, sin_t, esel, tri, dmask)


def _rope_tables(s):
    t = np.arange(s)
    quarter = HEAD_DIM // 4
    inv_freq = (ROPE_BASE ** (-np.arange(quarter, dtype=np.float32) / quarter)).astype(np.float32)
    lane = np.arange(LANES) % HEAD_DIM
    use_col = lane >= HEAD_DIM // 2
    second = (lane % (HEAD_DIM // 2)) >= quarter
    pos = np.where(use_col[None, :], (t % GRID_W)[:, None], (t // GRID_W)[:, None]).astype(np.float32)
    ang = (pos * inv_freq[lane % quarter][None, :]).astype(np.float64)
    sin = np.where(second[None, :], np.sin(ang), -np.sin(ang))
    return jnp.asarray(np.cos(ang), F32), jnp.asarray(sin, F32)


def _gate_tables():
    esel = np.zeros((2, 48, 2 * LANES), np.float32)
    for d in range(2):
        for ln in range(LANES):
            e = ln // HEAD_DIM
            esel[d, 4 + 2 * d + e, ln] = 1.0
            esel[d, 4 + 2 * d + e, LANES + ln] = -1.0
            esel[d, 8 + 2 * d + e, LANES + ln] = 1.0
    esel[:, 16:32, :] = esel[:, :16, :]
    esel[:, 32:, :] = esel[:, :16, :]
    t = np.arange(C_LEN)
    tri = np.stack([(t[None, :] <= t[:, None]), (t[None, :] >= t[:, None])])
    dmask = np.where(tri, 0.0, NEG).astype(np.float32)
    return jnp.asarray(esel, BF16), jnp.asarray(tri.astype(np.float32), BF16), jnp.asarray(dmask)


def _post_kernel(x_ref, ya_ref, yb_ref, yc_ref, mod_ref, wout_ref, g_ref, wg_ref, wu_ref, wd_ref, fg_ref, o_ref,
                 *, final, ff_chunk):
    y = jnp.concatenate([ya_ref[0, 0], ya_ref[0, 1], yb_ref[0, 0], yb_ref[0, 1], yb_ref[0, 2],
                         yc_ref[0, 0], yc_ref[0, 1], yc_ref[0, 2]], axis=1).astype(BF16)
    x1 = x_ref[0] + mod_ref[0, 2:3, :] * _dot(y, wout_ref[...])
    hm = _rms_modulate(x1, g_ref[...], mod_ref[0, 3:4, :], mod_ref[0, 4:5, :]).astype(BF16)
    acc = jnp.zeros_like(x1)
    for c in range(wg_ref.shape[1] // ff_chunk):
        cols = slice(c * ff_chunk, (c + 1) * ff_chunk)
        a = _silu(_dot(hm, wg_ref[:, cols])) * _dot(hm, wu_ref[:, cols])
        acc = acc + _dot(a.astype(BF16), wd_ref[cols, :])
    x2 = x1 + mod_ref[0, 5:6, :] * acc
    if final:
        ms = jnp.mean(x2 * x2, axis=-1, keepdims=True)
        x2 = x2 * lax.rsqrt(ms + EPS) * fg_ref[...]
    o_ref[0] = x2


def _post_call(x, ya, yb, yc, mod, wout, g, wg, wu, wd, fg, tm, final):
    b, s, d = x.shape
    dff = wg.shape[1]
    assert dff % MXU_N == 0
    per_sample = mod.shape[0] == b
    mod_map = (lambda i, j: (i, 0, 0)) if per_sample else (lambda i, j: (0, 0, 0))
    const = lambda i, j: (0, 0)
    once = dict(pipeline_mode=pl.Buffered(1))
    yt = lambda n: pl.BlockSpec((1, n, tm, LANES), lambda i, j: (i, 0, j, 0))
    return pl.pallas_call(
        functools.partial(_post_kernel, final=final, ff_chunk=MXU_N),
        grid=(b, s // tm),
        in_specs=[pl.BlockSpec((1, tm, d), lambda i, j: (i, j, 0)),
                  yt(2), yt(N_PAIRS), yt(N_PAIRS),
                  pl.BlockSpec((1, N_MOD, d), mod_map),
                  pl.BlockSpec((d, d), const, **once),
                  pl.BlockSpec((1, d), const),
                  pl.BlockSpec((d, dff), const, **once),
                  pl.BlockSpec((d, dff), const, **once),
                  pl.BlockSpec((dff, d), const, **once),
                  pl.BlockSpec((1, d), const)],
        out_specs=pl.BlockSpec((1, tm, d), lambda i, j: (i, j, 0)),
        out_shape=jax.ShapeDtypeStruct((b, s, d), F32),
        compiler_params=_cparams(("arbitrary", "arbitrary")),
        name="out_proj_swiglu",
    )(x, ya, yb, yc, mod, wout, g, wg, wu, wd, fg)


def _gate_perm():
    row = np.zeros(N_GATES, np.int32)
    for p in range(N_PAIRS):
        for k in range(4):
            for e in range(2):
                row[PAIR_GATES * p + 4 * (k % 2) + 2 * (k // 2) + e] = C_HEADS * k + 2 * p + e
    return row


def kernel(x, c, ctx, c_ctx, w_mod, b_mod, norm1_g, w_in, a_ln_g, a_ln_b, a_ws, a_bs, b_rpb,
           c_conv_w, c_conv_b, c_gate_b, w_out, norm2_g, w_gate, w_up, w_down, final_g):
    b, s, d = x.shape
    cl = ctx.shape[1]
    depth = w_mod.shape[0]
    n_main = (A_TILES + N_TILES) * LANES
    tm_proj, tm = min(1024, s), min(512, s)
    perm = _gate_perm()

    pad = (-(b + 1)) % 8
    cs = jnp.concatenate([c, c_ctx[None, :], jnp.zeros((pad, d), F32)], axis=0)
    mod_all = _mod_call(cs, w_mod.astype(BF16), b_mod)

    cos_t, sin_t = _rope_tables(s)
    esel, tri, dmask = _gate_tables()
    bias_all = _natten_bias(b_rpb.reshape((-1,) + b_rpb.shape[2:]) * LOG2E, s // GRID_W)
    avg = np.kron(np.eye(2), np.full((HEAD_DIM, HEAD_DIM), 1.0 / HEAD_DIM))
    avg = jnp.asarray(np.concatenate([avg, avg], axis=0), BF16)

    xc = ctx.reshape(1, b * cl, d)
    for l in range(depth):
        last = l == depth - 1
        mod = mod_all[l, :b].reshape(b, N_MOD, d)
        mod_c = mod_all[l, b:b + 1].reshape(1, N_MOD, d)

        w_main = jnp.concatenate([w_in[l][:, :n_main], w_in[l][:, n_main:][:, perm],
                                  jnp.zeros((d, LANES - N_GATES), F32)], axis=1).astype(BF16)
        gbt = jnp.broadcast_to(c_gate_b[l].reshape(-1)[perm][:, None], (N_GATES, tm_proj))
        g1 = norm1_g[l][None, :]

        mix_a = (a_ln_g[l].reshape(2, 1, LANES), a_ln_b[l].reshape(2, 1, LANES), a_ws[l].astype(BF16),
                 jnp.repeat(a_bs[l].reshape(2, 2, A_CHUNK).transpose(0, 2, 1), HEAD_DIM, axis=2), avg)
        z, zg, ya = _inproj_call(x, mod, g1, w_main, gbt, *mix_a, tm_proj)
        zc, zcg, yac = _inproj_call(xc, mod_c, g1, w_main, gbt, *mix_a, min(tm_proj, b * cl))

        yb = _mix_b_call(z, zc, bias_all, l)

        wconv = c_conv_w[l].reshape(3, 2 * N_PAIRS, LANES).transpose(1, 0, 2)
        bconv = c_conv_b[l].reshape(2 * N_PAIRS, 1, LANES)
        yc, ycc = _mix_c_call(z, zg, zc, zcg, wconv, bconv, cos_t, sin_t, esel, tri, dmask)

        weights = (w_out[l].astype(BF16), norm2_g[l][None, :], w_gate[l].astype(BF16), w_up[l].astype(BF16),
                   w_down[l].astype(BF16), final_g[None, :])
        x = _post_call(x, ya, yb, yc, mod, *weights, tm, last)
        if not last:
            ybc = _mix_b_ctx_call(zc, b)
            xc = _post_call(xc, yac, ybc, ycc, mod_c, *weights, min(tm, b * cl), False)
    return x
```

```python
import functools

import numpy as np
import jax
import jax.numpy as jnp
from jax import lax
from jax.experimental import pallas as pl
from jax.experimental.pallas import tpu as pltpu

F32 = jnp.float32
BF16 = jnp.bfloat16

LANES = 128
MXU_N = 256
HEAD_DIM = 64
GRID_W = 64
A_GROUPS = 4
A_CHUNK = 128
B_HEADS = 6
C_HEADS = 6
WIN_ROWS = 8
WIN_COLS = 16
ROPE_BASE = 10000.0
EPS = 1e-6
N_MOD = 6
NEG = -1e30
LOG2E = float(np.log2(np.e))

A_TILES = 4
T_BQ, T_BK, T_BV, T_CQ, T_CK, T_CV, T_CO = 0, 3, 6, 9, 12, 15, 18
N_TILES = 21
N_PAIRS = 3
N_GATES = 4 * C_HEADS
PAIR_GATES = N_GATES // N_PAIRS
Q_ROWS = 4
K_ROWS = Q_ROWS + WIN_ROWS
C_LEN = 256

VMEM_LIMIT = 56 * 1024 * 1024


def _cparams(sem):
    return pltpu.CompilerParams(dimension_semantics=sem, vmem_limit_bytes=VMEM_LIMIT)


def _dot(a, b):
    return jnp.dot(a, b, preferred_element_type=F32)


def _dot_nt(a, b):
    return lax.dot_general(a, b, (((1,), (1,)), ((), ())), preferred_element_type=F32)


def _split2(x):
    h = x.astype(BF16)
    return h, (x - h.astype(F32)).astype(BF16)


def _dot2_r(x, sel2):
    h, l = _split2(x)
    return _dot(jnp.concatenate([h, l], axis=1), sel2)


def _stack3(x):
    h = x.astype(BF16).astype(F32)
    r = x - h
    m = r.astype(BF16).astype(F32)
    return jnp.concatenate([h, m, r - m], axis=0).astype(BF16)


def _silu(x):
    h = 0.5 * x
    return h + h * jnp.tanh(h)


def _sigmoid(x):
    return 0.5 + 0.5 * jnp.tanh(0.5 * x)


def _gelu_tanh(x):
    return 0.5 * x * (1.0 + jnp.tanh(np.sqrt(2.0 / np.pi).astype(np.float32) * (x + 0.044715 * (x * x * x))))


def _log_sigmoid(x):
    return jnp.minimum(x, 0.0) - jnp.log1p(jnp.exp(-jnp.abs(x)))


def _rms_modulate(x, g, shift, scale):
    ms = jnp.mean(x * x, axis=-1, keepdims=True)
    return (x * lax.rsqrt(ms + EPS) * g) * (1.0 + scale) + shift


def _mod_kernel(c_ref, w_ref, b_ref, o_ref):
    o_ref[0] = _dot(_silu(c_ref[...]).astype(BF16), w_ref[0]) + b_ref[0, 0]


def _mod_call(cs, w_mod, b_mod):
    depth, d, nd = w_mod.shape
    rows = cs.shape[0]
    return pl.pallas_call(
        _mod_kernel,
        grid=(depth, nd // d),
        in_specs=[pl.BlockSpec((rows, d), lambda l, n: (0, 0)),
                  pl.BlockSpec((1, d, d), lambda l, n: (l, 0, n)),
                  pl.BlockSpec((1, 1, 1, d), lambda l, n: (l, n, 0, 0))],
        out_specs=pl.BlockSpec((1, rows, d), lambda l, n: (l, 0, n)),
        out_shape=jax.ShapeDtypeStruct((depth, rows, nd), F32),
        compiler_params=_cparams(("arbitrary", "arbitrary")),
        name="adaln_mod",
    )(cs, w_mod, b_mod.reshape(depth, nd // d, 1, d))


def _gmlp(u, v, j, lng_ref, lnb_ref, ws_ref, bsx_ref, avg):
    lane = lax.broadcasted_iota(jnp.int32, (A_CHUNK, LANES), 1)
    v = _gelu_tanh(v)
    dv = v - _dot2_r(v, avg)
    var = _dot2_r(dv * dv, avg)
    vn = (dv * lax.rsqrt(var + EPS) * lng_ref[j] + lnb_ref[j]).astype(BF16)
    u = _gelu_tanh(u)
    outs = []
    for c in range(u.shape[0] // A_CHUNK):
        rows = slice(c * A_CHUNK, (c + 1) * A_CHUNK)
        s = jnp.where(lane < HEAD_DIM, _dot(ws_ref[2 * j], vn[rows]), _dot(ws_ref[2 * j + 1], vn[rows])) + bsx_ref[j]
        outs.append(u[rows] * s)
    return jnp.concatenate(outs, axis=0)


def _inproj_kernel(x_ref, mod_ref, g_ref, w_ref, gbt_ref, lng_ref, lnb_ref, ws_ref, bsx_ref, avg_ref,
                   z_ref, zgt_ref, ya_ref):
    xm = _rms_modulate(x_ref[0], g_ref[...], mod_ref[0, 0:1, :], mod_ref[0, 1:2, :])
    xb = xm.astype(BF16)
    u = _dot(xb, w_ref[:, :MXU_N])
    v = _dot(xb, w_ref[:, MXU_N:2 * MXU_N])
    for j in range((N_TILES + 1) // 2):
        r = _dot(xb, w_ref[:, (A_TILES // 2 + j) * MXU_N:(A_TILES // 2 + j + 1) * MXU_N])
        z_ref[0, 2 * j] = r[:, :LANES]
        if 2 * j + 1 < N_TILES:
            z_ref[0, 2 * j + 1] = r[:, LANES:]
    g = r[:, LANES:].T[:N_GATES, :] + gbt_ref[...]
    sub = lax.broadcasted_iota(jnp.int32, g.shape, 0)
    zgt_ref[0] = jnp.where(sub % PAIR_GATES >= PAIR_GATES // 2, _log_sigmoid(g), g)
    for j in range(2):
        cols = slice(j * LANES, (j + 1) * LANES)
        ya_ref[0, j] = _gmlp(u[:, cols], v[:, cols], j, lng_ref, lnb_ref, ws_ref, bsx_ref, avg_ref[...])


def _inproj_call(x, mod, g, w, gbt, lng, lnb, ws, bsx, avg, tm):
    b, s, d = x.shape
    per_sample = mod.shape[0] == b
    mod_map = (lambda i, j: (i, 0, 0)) if per_sample else (lambda i, j: (0, 0, 0))
    const = lambda i, j: (0, 0)
    c3 = lambda i, j: (0, 0, 0)
    return pl.pallas_call(
        _inproj_kernel,
        grid=(b, s // tm),
        in_specs=[pl.BlockSpec((1, tm, d), lambda i, j: (i, j, 0)),
                  pl.BlockSpec((1, N_MOD, d), mod_map),
                  pl.BlockSpec((1, d), const),
                  pl.BlockSpec((d, (A_TILES + N_TILES + 1) * LANES), const, pipeline_mode=pl.Buffered(1)),
                  pl.BlockSpec((N_GATES, tm), const),
                  pl.BlockSpec((2, 1, LANES), c3),
                  pl.BlockSpec((2, 1, LANES), c3),
                  pl.BlockSpec((A_GROUPS, A_CHUNK, A_CHUNK), c3),
                  pl.BlockSpec((2, A_CHUNK, LANES), c3),
                  pl.BlockSpec((2 * LANES, LANES), const)],
        out_specs=[pl.BlockSpec((1, N_TILES, tm, LANES), lambda i, j: (i, 0, j, 0)),
                   pl.BlockSpec((1, N_GATES, tm), lambda i, j: (i, 0, j)),
                   pl.BlockSpec((1, 2, tm, LANES), lambda i, j: (i, 0, j, 0))],
        out_shape=[jax.ShapeDtypeStruct((b, N_TILES, s, LANES), F32),
                   jax.ShapeDtypeStruct((b, N_GATES, s), F32),
                   jax.ShapeDtypeStruct((b, 2, s, LANES), F32)],
        compiler_params=_cparams(("arbitrary", "arbitrary")),
        name="in_proj",
    )(x, mod, g, w, gbt[:, :tm], lng, lnb, ws, bsx, avg)


def _mix_b_kernel(q_ref, k_ref, v_ref, kc_ref, vc_ref, bias_ref, o_ref, kb_ref, vb_ref, *, n_rows):
    n_blocks = n_rows // Q_ROWS
    nq = Q_ROWS * GRID_W
    nk = K_ROWS * GRID_W
    kb_ref[...] = k_ref[0, 0].astype(BF16)
    vb_ref[...] = v_ref[0, 0].astype(BF16)
    kc = kc_ref[0, 0].astype(BF16)
    vc = vc_ref[0, 0].astype(BF16)
    lane = lax.broadcasted_iota(jnp.int32, (nq, LANES), 1)

    def block_pair(i, carry):
        units = []
        for rb in (2 * i, 2 * i + 1):
            q_start = pl.multiple_of(rb * nq, nq)
            k_start = pl.multiple_of(jnp.clip(rb * Q_ROWS - WIN_ROWS // 2, 0, n_rows - K_ROWS) * GRID_W, nq)
            cls = jnp.where(rb == 0, 0, jnp.where(rb == n_blocks - 1, 2, 1))
            q = q_ref[0, 0, pl.ds(q_start, nq), :] * (HEAD_DIM ** -0.5 * LOG2E)
            kw = kb_ref[pl.ds(k_start, nk), :]
            for e in range(2):
                qe = jnp.where((lane < HEAD_DIM) == (e == 0), q, 0.0).astype(BF16)
                units.append((q_start, k_start, _dot_nt(qe, kw) + bias_ref[0, e, cls], _dot_nt(qe, kc)))
        probs = []
        for _, _, s_w, s_c in units:
            m = jnp.maximum(jnp.max(s_w, axis=-1, keepdims=True), jnp.max(s_c, axis=-1, keepdims=True))
            p_w = jnp.exp2(s_w - m)
            p_c = jnp.exp2(s_c - m)
            den = jnp.sum(p_w, axis=-1, keepdims=True) + jnp.sum(p_c, axis=-1, keepdims=True)
            probs.append((p_w.astype(BF16), p_c.astype(BF16), den))
        outs = [(_dot(p_w, vb_ref[pl.ds(k_start, nk), :]) + _dot(p_c, vc)) / den
                for (_, k_start, _, _), (p_w, p_c, den) in zip(units, probs)]
        for b2 in range(2):
            o_ref[0, 0, pl.ds(units[2 * b2][0], nq), :] = jnp.where(lane < HEAD_DIM, outs[2 * b2], outs[2 * b2 + 1])
        return carry

    lax.fori_loop(0, n_blocks // 2, block_pair, 0)


def _mix_b_call(z, zc, bias, layer):
    b, _, s, _ = z.shape
    cl = zc.shape[2] // b
    n_rows = s // GRID_W
    assert n_rows % (2 * Q_ROWS) == 0 and n_rows >= K_ROWS
    zt = lambda t: pl.BlockSpec((1, 1, s, LANES), lambda p, i: (i, t + p, 0, 0))
    ct = lambda t: pl.BlockSpec((1, 1, cl, LANES), lambda p, i: (0, t + p, i, 0))
    return pl.pallas_call(
        functools.partial(_mix_b_kernel, n_rows=n_rows),
        grid=(N_PAIRS, b),
        in_specs=[zt(T_BQ), zt(T_BK), zt(T_BV), ct(T_BK), ct(T_BV),
                  pl.BlockSpec((1, 2, 3, Q_ROWS * GRID_W, K_ROWS * GRID_W), lambda p, i: (layer * N_PAIRS + p, 0, 0, 0, 0))],
        out_specs=pl.BlockSpec((1, 1, s, LANES), lambda p, i: (i, p, 0, 0)),
        out_shape=jax.ShapeDtypeStruct((b, N_PAIRS, s, LANES), F32),
        scratch_shapes=[pltpu.VMEM((s, LANES), BF16), pltpu.VMEM((s, LANES), BF16)],
        compiler_params=_cparams(("arbitrary", "arbitrary")),
        name="mix_b_natten",
    )(z, z, z, zc, zc, bias)


def _mix_b_ctx_kernel(q_ref, k_ref, v_ref, o_ref):
    q = q_ref[0, 0] * (HEAD_DIM ** -0.5)
    k = k_ref[0, 0].astype(BF16)
    v = v_ref[0, 0].astype(BF16)
    lane = lax.broadcasted_iota(jnp.int32, q.shape, 1)
    outs = []
    for e in range(2):
        qe = jnp.where((lane < HEAD_DIM) == (e == 0), q, 0.0).astype(BF16)
        s = _dot_nt(qe, k)
        p = jnp.exp(s - jnp.max(s, axis=-1, keepdims=True))
        outs.append(_dot(p.astype(BF16), v) / jnp.sum(p, axis=-1, keepdims=True))
    o_ref[0, 0] = jnp.where(lane < HEAD_DIM, outs[0], outs[1])


def _mix_b_ctx_call(zc, b):
    cl = zc.shape[2] // b
    blk = lambda t: pl.BlockSpec((1, 1, cl, LANES), lambda i, p: (0, t + p, i, 0))
    return pl.pallas_call(
        _mix_b_ctx_kernel,
        grid=(b, N_PAIRS),
        in_specs=[blk(T_BQ), blk(T_BK), blk(T_BV)],
        out_specs=blk(0),
        out_shape=jax.ShapeDtypeStruct((1, N_PAIRS, b * cl, LANES), F32),
        compiler_params=_cparams(("arbitrary", "arbitrary")),
        name="mix_b_ctx",
    )(zc, zc, zc)


def _natten_bias(rpb, n_rows):
    h = rpb.shape[0]
    padded = jnp.pad(rpb, ((0, 0), (0, 0), (GRID_W, GRID_W)))
    toe = jnp.stack([padded[:, :, GRID_W + WIN_COLS - 1 - c:2 * GRID_W + WIN_COLS - 1 - c] for c in range(GRID_W)], axis=2)
    c = np.arange(GRID_W)[:, None]
    kc = np.arange(GRID_W)[None, :]
    cs = np.clip(c - WIN_COLS // 2, 0, GRID_W - WIN_COLS)
    col_ok = (kc >= cs) & (kc < cs + WIN_COLS)
    toe = jnp.where(col_ok[None, None], toe, NEG)
    neg_block = jnp.full((h, GRID_W, GRID_W), NEG, F32)
    classes = []
    for r0 in (0, Q_ROWS, n_rows - Q_ROWS):
        k0 = int(np.clip(r0 - WIN_ROWS // 2, 0, n_rows - K_ROWS))
        q_rows = []
        for i in range(Q_ROWS):
            r = r0 + i
            rs = int(np.clip(r - WIN_ROWS // 2, 0, n_rows - WIN_ROWS))
            blocks = []
            for j in range(K_ROWS):
                kr = k0 + j
                blocks.append(toe[:, kr - r + WIN_ROWS - 1] if rs <= kr < rs + WIN_ROWS else neg_block)
            q_rows.append(jnp.concatenate(blocks, axis=-1))
        classes.append(jnp.concatenate(q_rows, axis=1))
    bias = jnp.stack(classes, axis=1)
    return bias.reshape(h // 2, 2, 3, Q_ROWS * GRID_W, K_ROWS * GRID_W)


def _mlstm_chunks(items, dmask_ref, head0, ones_blk, bd_mask):
    n_l = items[0][1].shape[0]
    half = n_l // 2
    zeros = jnp.zeros((half, half), BF16)

    early = []
    for (direction, qb, q2, kt, v, v_heads, fc, w, b_rows, state) in items:
        st, m_pl = state
        s_heads = _dot(q2, kt)
        inter = _dot(qb, st.astype(BF16))
        w_max = jnp.max(w, axis=0, keepdims=True)
        a_s = jnp.exp2(w - w_max)
        upd = jnp.where(bd_mask, _dot(kt, jnp.concatenate([a_s * v, a_s], axis=1).astype(BF16)), 0.0)
        f_tot = fc[n_l - 1:n_l, :] if direction == 0 else fc[0:1, :]
        mw = jnp.maximum(m_pl, w_max)
        a_prev = jnp.exp2(m_pl - mw)
        a_new = jnp.exp2(w_max - mw)
        st_new = jnp.concatenate([a_prev, a_prev], axis=1) * st + jnp.concatenate([a_new, a_new], axis=1) * upd
        early.append((s_heads, inter, m_pl, st_new, f_tot + mw))

    weights = []
    for (direction, qb, q2, kt, v, v_heads, fc, w, b_rows, state), (s_heads, _, _, _, _) in zip(items, early):
        first = slice(0, half) if direction == 0 else slice(half, n_l)
        last = slice(half, n_l) if direction == 0 else slice(0, half)
        p_heads, cm_heads = [], []
        for e in range(2):
            s = s_heads[e * n_l:(e + 1) * n_l]
            bm_a = b_rows[e][:, first] + dmask_ref[direction, first, first]
            bm_b = b_rows[e] + dmask_ref[direction, last, :]
            cm_a = jnp.max(bm_a, axis=-1, keepdims=True)
            cm_b = jnp.max(bm_b, axis=-1, keepdims=True)
            p_a = (s[first, first] * jnp.exp2(bm_a - cm_a)).astype(BF16)
            p_b = (s[last, :] * jnp.exp2(bm_b - cm_b)).astype(BF16)
            if direction == 0:
                p_heads.append(jnp.concatenate([jnp.concatenate([p_a, zeros], axis=1), p_b], axis=0))
                cm_heads.append(jnp.concatenate([cm_a, cm_b], axis=0))
            else:
                p_heads.append(jnp.concatenate([p_b, jnp.concatenate([zeros, p_a], axis=1)], axis=0))
                cm_heads.append(jnp.concatenate([cm_b, cm_a], axis=0))
        weights.append((jnp.concatenate(p_heads, axis=1), jnp.where(head0, cm_heads[0], cm_heads[1])))

    intras = [_dot(p, jnp.concatenate([item[5], ones_blk], axis=1)) for item, (p, _) in zip(items, weights)]

    results = []
    for item, (_, inter, m_pl, st_new, m_new), (_, cm_pl), intra in zip(items, early, weights, intras):
        fc = item[6]
        mx = jnp.maximum(m_pl, cm_pl)
        w_intra = jnp.exp2(cm_pl - mx)
        w_inter = jnp.exp2(m_pl - mx)
        num = w_inter * inter[:, :LANES] + w_intra * intra[:, :LANES]
        den = w_inter * inter[:, LANES:] + w_intra * intra[:, LANES:]
        results.append((num / jnp.maximum(jnp.abs(den), jnp.exp2(-(fc + mx))), st_new, m_new))
    return results


def _mix_c_kernel(zq_ref, zk_ref, zv_ref, rg_ref, cq_ref, ck_ref, cv_ref, crg_ref,
                  wq_ref, wk_ref, bq_ref, bk_ref, cos_ref, sin_ref, esel_ref, tri_ref, dmask_ref,
                  of_ref, ob_ref, ocf_ref, ocb_ref, qs_ref, q2_ref, kt_ref, vh_ref, fw_ref, br_ref):
    s_len = zq_ref.shape[2]
    n_l = C_LEN
    n_chunks = s_len // n_l
    lane = lax.broadcasted_iota(jnp.int32, (n_l, LANES), 1)
    row = lax.broadcasted_iota(jnp.int32, (n_l, LANES), 0)
    k_scale = HEAD_DIM ** -0.5

    def conv_silu(x_ref, t0, total, w_ref_, b_ref_):
        x = x_ref[0, 0, pl.ds(t0, n_l), :]
        if total == n_l:
            prev_row = next_row = jnp.zeros((1, LANES), F32)
        else:
            prev_row = x_ref[0, 0, pl.ds(jnp.maximum(t0 - 1, 0), 1), :] * (t0 > 0).astype(F32)
            next_row = x_ref[0, 0, pl.ds(jnp.minimum(t0 + n_l, total - 1), 1), :] * (t0 + n_l < total).astype(F32)
        x_prev = jnp.where(row == 0, prev_row, pltpu.roll(x, 1, axis=0))
        x_next = jnp.where(row == n_l - 1, next_row, pltpu.roll(x, n_l - 1, axis=0))
        y = x_prev * w_ref_[0, 0:1, :] + x * w_ref_[0, 1:2, :] + x_next * w_ref_[0, 2:3, :] + b_ref_[0]
        return _silu(y)

    def rope(x, t0):
        cos = cos_ref[pl.ds(t0, n_l), :]
        sin = sin_ref[pl.ds(t0, n_l), :]
        first = (lane % (HEAD_DIM // 2)) < (HEAD_DIM // 4)
        partner = jnp.where(first, pltpu.roll(x, LANES - HEAD_DIM // 4, axis=1), pltpu.roll(x, HEAD_DIM // 4, axis=1))
        return x * cos + partner * sin

    def qk_prep(q_in, k_in, t0, total, with_rope):
        q = conv_silu(q_in, t0, total, wq_ref, bq_ref)
        k = conv_silu(k_in, t0, total, wk_ref, bk_ref)
        if with_rope:
            q, k = rope(q, t0), rope(k, t0)
        return q, (k * k_scale).T.astype(BF16)

    def gate_prep(rg, d):
        n_g = rg.shape[0]
        f3 = _dot(_stack3(rg), tri_ref[1 - d])
        f_row = f3[:n_g] + f3[n_g:2 * n_g] + f3[2 * n_g:]
        b_rows = [rg[2 * d + e:2 * d + e + 1, :] - f_row[4 + 2 * d + e:5 + 2 * d + e, :] for e in range(2)]
        fw = lax.dot_general(_stack3(jnp.concatenate([f_row, rg], axis=0)), esel_ref[d], (((0,), (0,)), ((), ())),
                             preferred_element_type=F32)
        return fw[:, :LANES], fw[:, LANES:], b_rows

    head0 = lane < HEAD_DIM

    head_lanes = [jnp.where((lane[0:1, :] < HEAD_DIM) == (e == 0), 1.0, 0.0).astype(BF16) for e in range(2)]

    def per_head(x):
        xb = x.astype(BF16)
        return jnp.concatenate([xb * head_lanes[0], xb * head_lanes[1]], axis=0)

    ones_blk = jnp.concatenate([jnp.where(head0, 1.0, 0.0), jnp.where(head0, 0.0, 1.0)], axis=0).astype(BF16)
    bd_r = lax.broadcasted_iota(jnp.int32, (LANES, 2 * LANES), 0)
    bd_c = lax.broadcasted_iota(jnp.int32, (LANES, 2 * LANES), 1)
    bd_mask = (bd_r < HEAD_DIM) == ((bd_c % LANES) < HEAD_DIM)
    consts = (head0, ones_blk, bd_mask)

    cq, ckt = qk_prep(cq_ref, ck_ref, 0, n_l, False)
    cv = cv_ref[0, 0]
    crg = crg_ref[0, 0] * LOG2E
    items = []
    for d in range(2):
        items.append((d, cq.astype(BF16), per_head(cq), ckt, cv, per_head(cv), *gate_prep(crg, d),
                      (jnp.zeros((LANES, 2 * LANES), F32), jnp.zeros((1, LANES), F32))))
    res = _mlstm_chunks(items, dmask_ref, *consts)
    ocf_ref[0, 0] = res[0][0]
    ocb_ref[0, 0] = res[1][0]
    carry = [(st, m_pl) for _, st, m_pl in res]

    def prep(c, carry):
        t0 = pl.multiple_of(c * n_l, n_l)
        rows = pl.ds(t0, n_l)
        q, kt_ref[:, rows] = qk_prep(zq_ref, zk_ref, t0, s_len, True)
        qs_ref[rows, :] = q.astype(BF16)
        q2_ref[:, rows, :] = per_head(q).reshape(2, n_l, LANES)
        vh_ref[:, rows, :] = per_head(zv_ref[0, 0, rows, :]).reshape(2, n_l, LANES)
        rg = rg_ref[0, 0, :, rows] * LOG2E
        for d in range(2):
            fc, w, b_rows = gate_prep(rg, d)
            fw_ref[d, rows, :] = jnp.concatenate([fc, w], axis=1)
            for e in range(2):
                br_ref[2 * d + e:2 * d + e + 1, rows] = b_rows[e]
        br_ref[PAIR_GATES // 2:, rows] = jnp.zeros((PAIR_GATES // 2, n_l), F32)
        return carry

    lax.fori_loop(0, n_chunks, prep, 0, unroll=4)

    def step(i, carry):
        items, all_rows = [], []
        for d in range(2):
            c = i if d == 0 else n_chunks - 1 - i
            rows = pl.ds(pl.multiple_of(c * n_l, n_l), n_l)
            fw = fw_ref[d, rows, :]
            br = br_ref[:, rows]
            b_rows = [br[2 * d + e:2 * d + e + 1, :] for e in range(2)]
            items.append((d, qs_ref[rows, :], q2_ref[:, rows, :].reshape(2 * n_l, LANES), kt_ref[:, rows],
                          zv_ref[0, 0, rows, :], vh_ref[:, rows, :].reshape(2 * n_l, LANES),
                          fw[:, :LANES], fw[:, LANES:], b_rows, carry[d]))
            all_rows.append(rows)
        res = _mlstm_chunks(items, dmask_ref, *consts)
        of_ref[0, 0, all_rows[0], :] = res[0][0]
        ob_ref[0, 0, all_rows[1], :] = res[1][0]
        return tuple((st, m_pl) for _, st, m_pl in res)

    lax.fori_loop(0, n_chunks, step, tuple(carry), unroll=2)


def _mix_c_call(z, zg, zc, zcg, wconv, bconv, cos_t, sin_t, esel, tri, dmask):
    b, _, s, _ = z.shape
    cl = zc.shape[2] // b
    assert cl == C_LEN and s % (2 * C_LEN) == 0
    once = dict(pipeline_mode=pl.Buffered(1))
    zt = lambda t: pl.BlockSpec((1, 1, s, LANES), lambda i, p: (i, t + p, 0, 0))
    ct = lambda t: pl.BlockSpec((1, 1, cl, LANES), lambda i, p: (0, t + p, i, 0))
    return pl.pallas_call(
        _mix_c_kernel,
        grid=(b, N_PAIRS),
        in_specs=[zt(T_CQ), zt(T_CK), zt(T_CV),
                  pl.BlockSpec((1, 1, PAIR_GATES, s), lambda i, p: (i, p, 0, 0)),
                  ct(T_CQ), ct(T_CK), ct(T_CV),
                  pl.BlockSpec((1, 1, PAIR_GATES, cl), lambda i, p: (0, p, 0, i)),
                  pl.BlockSpec((1, 3, LANES), lambda i, p: (p, 0, 0)),
                  pl.BlockSpec((1, 3, LANES), lambda i, p: (N_PAIRS + p, 0, 0)),
                  pl.BlockSpec((1, 1, LANES), lambda i, p: (p, 0, 0)),
                  pl.BlockSpec((1, 1, LANES), lambda i, p: (N_PAIRS + p, 0, 0)),
                  pl.BlockSpec((s, LANES), lambda i, p: (0, 0), **once),
                  pl.BlockSpec((s, LANES), lambda i, p: (0, 0), **once),
                  pl.BlockSpec((2, 48, 2 * LANES), lambda i, p: (0, 0, 0), **once),
                  pl.BlockSpec((2, C_LEN, C_LEN), lambda i, p: (0, 0, 0), **once),
                  pl.BlockSpec((2, C_LEN, C_LEN), lambda i, p: (0, 0, 0), **once)],
        out_specs=[pl.BlockSpec((1, 1, s, LANES), lambda i, p: (i, p, 0, 0)),
                   pl.BlockSpec((1, 1, s, LANES), lambda i, p: (i, p, 0, 0)),
                   pl.BlockSpec((1, 1, cl, LANES), lambda i, p: (0, p, i, 0)),
                   pl.BlockSpec((1, 1, cl, LANES), lambda i, p: (0, p, i, 0))],
        out_shape=[jax.ShapeDtypeStruct((b, N_PAIRS, s, LANES), F32),
                   jax.ShapeDtypeStruct((b, N_PAIRS, s, LANES), F32),
                   jax.ShapeDtypeStruct((1, N_PAIRS, b * cl, LANES), F32),
                   jax.ShapeDtypeStruct((1, N_PAIRS, b * cl, LANES), F32)],
        scratch_shapes=[pltpu.VMEM((s, LANES), BF16), pltpu.VMEM((2, s, LANES), BF16), pltpu.VMEM((LANES, s), BF16),
                        pltpu.VMEM((2, s, LANES), BF16), pltpu.VMEM((2, s, 2 * LANES), F32), pltpu.VMEM((PAIR_GATES, s), F32)],
        compiler_params=_cparams(("arbitrary", "arbitrary")),
        name="mix_c_mlstm",
    )(z, z, z, zg.reshape(b, N_PAIRS, PAIR_GATES, s), zc, zc, zc, zcg.reshape(1, N_PAIRS, PAIR_GATES, b * cl),
      wconv, wconv, bconv, bconv, cos_t, sin_t, esel, tri, dmask)


def _rope_tables(s):
    t = np.arange(s)
    quarter = HEAD_DIM // 4
    inv_freq = (ROPE_BASE ** (-np.arange(quarter, dtype=np.float32) / quarter)).astype(np.float32)
    lane = np.arange(LANES) % HEAD_DIM
    use_col = lane >= HEAD_DIM // 2
    second = (lane % (HEAD_DIM // 2)) >= quarter
    pos = np.where(use_col[None, :], (t % GRID_W)[:, None], (t // GRID_W)[:, None]).astype(np.float32)
    ang = (pos * inv_freq[lane % quarter][None, :]).astype(np.float64)
    sin = np.where(second[None, :], np.sin(ang), -np.sin(ang))
    return jnp.asarray(np.cos(ang), F32), jnp.asarray(sin, F32)


def _gate_tables():
    esel = np.zeros((2, 48, 2 * LANES), np.float32)
    for d in range(2):
        for ln in range(LANES):
            e = ln // HEAD_DIM
            esel[d, 4 + 2 * d + e, ln] = 1.0
            esel[d, 4 + 2 * d + e, LANES + ln] = -1.0
            esel[d, 8 + 2 * d + e, LANES + ln] = 1.0
    esel[:, 16:32, :] = esel[:, :16, :]
    esel[:, 32:, :] = esel[:, :16, :]
    t = np.arange(C_LEN)
    tri = np.stack([(t[None, :] <= t[:, None]), (t[None, :] >= t[:, None])])
    dmask = np.where(tri, 0.0, NEG).astype(np.float32)
    return jnp.asarray(esel, BF16), jnp.asarray(tri.astype(np.float32), BF16), jnp.asarray(dmask)


def _post_kernel(x_ref, ya_ref, yb_ref, ycf_ref, ycb_ref, zo_ref, mod_ref, wout_ref, g_ref, wg_ref, wu_ref, wd_ref,
                 fg_ref, o_ref, *, final, ff_chunk):
    yc = [(ycf_ref[0, t] + ycb_ref[0, t]) * _sigmoid(zo_ref[0, t]) for t in range(N_PAIRS)]
    y = jnp.concatenate([ya_ref[0, 0], ya_ref[0, 1], yb_ref[0, 0], yb_ref[0, 1], yb_ref[0, 2]] + yc,
                        axis=1).astype(BF16)
    x1 = x_ref[0] + mod_ref[0, 2:3, :] * _dot(y, wout_ref[...])
    hm = _rms_modulate(x1, g_ref[...], mod_ref[0, 3:4, :], mod_ref[0, 4:5, :]).astype(BF16)
    acc = jnp.zeros_like(x1)
    for c in range(wg_ref.shape[1] // ff_chunk):
        cols = slice(c * ff_chunk, (c + 1) * ff_chunk)
        a = _silu(_dot(hm, wg_ref[:, cols])) * _dot(hm, wu_ref[:, cols])
        acc = acc + _dot(a.astype(BF16), wd_ref[cols, :])
    x2 = x1 + mod_ref[0, 5:6, :] * acc
    if final:
        ms = jnp.mean(x2 * x2, axis=-1, keepdims=True)
        x2 = x2 * lax.rsqrt(ms + EPS) * fg_ref[...]
    o_ref[0] = x2


def _post_call(x, ya, yb, ycf, ycb, z, mod, wout, g, wg, wu, wd, fg, tm, final):
    b, s, d = x.shape
    dff = wg.shape[1]
    assert dff % MXU_N == 0
    per_sample = mod.shape[0] == b
    mod_map = (lambda i, j: (i, 0, 0)) if per_sample else (lambda i, j: (0, 0, 0))
    const = lambda i, j: (0, 0)
    once = dict(pipeline_mode=pl.Buffered(1))
    yt = lambda n: pl.BlockSpec((1, n, tm, LANES), lambda i, j: (i, 0, j, 0))
    return pl.pallas_call(
        functools.partial(_post_kernel, final=final, ff_chunk=MXU_N),
        grid=(b, s // tm),
        in_specs=[pl.BlockSpec((1, tm, d), lambda i, j: (i, j, 0)),
                  yt(2), yt(N_PAIRS), yt(N_PAIRS), yt(N_PAIRS),
                  pl.BlockSpec((1, N_PAIRS, tm, LANES), lambda i, j: (i, T_CO // N_PAIRS, j, 0)),
                  pl.BlockSpec((1, N_MOD, d), mod_map),
                  pl.BlockSpec((d, d), const, **once),
                  pl.BlockSpec((1, d), const),
                  pl.BlockSpec((d, dff), const, **once),
                  pl.BlockSpec((d, dff), const, **once),
                  pl.BlockSpec((dff, d), const, **once),
                  pl.BlockSpec((1, d), const)],
        out_specs=pl.BlockSpec((1, tm, d), lambda i, j: (i, j, 0)),
        out_shape=jax.ShapeDtypeStruct((b, s, d), F32),
        compiler_params=_cparams(("arbitrary", "arbitrary")),
        name="out_proj_swiglu",
    )(x, ya, yb, ycf, ycb, z, mod, wout, g, wg, wu, wd, fg)


def _gate_perm():
    row = np.zeros(N_GATES, np.int32)
    for p in range(N_PAIRS):
        for k in range(4):
            for e in range(2):
                row[PAIR_GATES * p + 4 * (k % 2) + 2 * (k // 2) + e] = C_HEADS * k + 2 * p + e
    return row


def kernel(x, c, ctx, c_ctx, w_mod, b_mod, norm1_g, w_in, a_ln_g, a_ln_b, a_ws, a_bs, b_rpb,
           c_conv_w, c_conv_b, c_gate_b, w_out, norm2_g, w_gate, w_up, w_down, final_g):
    b, s, d = x.shape
    cl = ctx.shape[1]
    depth = w_mod.shape[0]
    n_main = (A_TILES + N_TILES) * LANES
    tm_proj, tm = min(1024, s), min(512, s)
    perm = _gate_perm()

    pad = (-(b + 1)) % 8
    cs = jnp.concatenate([c, c_ctx[None, :], jnp.zeros((pad, d), F32)], axis=0)
    mod_all = _mod_call(cs, w_mod.astype(BF16), b_mod)

    cos_t, sin_t = _rope_tables(s)
    esel, tri, dmask = _gate_tables()
    bias_all = _natten_bias(b_rpb.reshape((-1,) + b_rpb.shape[2:]) * LOG2E, s // GRID_W)
    avg = np.kron(np.eye(2), np.full((HEAD_DIM, HEAD_DIM), 1.0 / HEAD_DIM))
    avg = jnp.asarray(np.concatenate([avg, avg], axis=0), BF16)

    xc = ctx.reshape(1, b * cl, d)
    for l in range(depth):
        last = l == depth - 1
        mod = mod_all[l, :b].reshape(b, N_MOD, d)
        mod_c = mod_all[l, b:b + 1].reshape(1, N_MOD, d)

        w_main = jnp.concatenate([w_in[l][:, :n_main], w_in[l][:, n_main:][:, perm],
                                  jnp.zeros((d, LANES - N_GATES), F32)], axis=1).astype(BF16)
        gbt = jnp.broadcast_to(c_gate_b[l].reshape(-1)[perm][:, None], (N_GATES, tm_proj))
        g1 = norm1_g[l][None, :]

        mix_a = (a_ln_g[l].reshape(2, 1, LANES), a_ln_b[l].reshape(2, 1, LANES), a_ws[l].astype(BF16),
                 jnp.repeat(a_bs[l].reshape(2, 2, A_CHUNK).transpose(0, 2, 1), HEAD_DIM, axis=2), avg)
        z, zg, ya = _inproj_call(x, mod, g1, w_main, gbt, *mix_a, tm_proj)
        zc, zcg, yac = _inproj_call(xc, mod_c, g1, w_main, gbt, *mix_a, min(tm_proj, b * cl))

        yb = _mix_b_call(z, zc, bias_all, l)

        wconv = c_conv_w[l].reshape(3, 2 * N_PAIRS, LANES).transpose(1, 0, 2)
        bconv = c_conv_b[l].reshape(2 * N_PAIRS, 1, LANES)
        ycf, ycb, yccf, yccb = _mix_c_call(z, zg, zc, zcg, wconv, bconv, cos_t, sin_t, esel, tri, dmask)

        weights = (w_out[l].astype(BF16), norm2_g[l][None, :], w_gate[l].astype(BF16), w_up[l].astype(BF16),
                   w_down[l].astype(BF16), final_g[None, :])
        x = _post_call(x, ya, yb, ycf, ycb, z, mod, *weights, tm, last)
        if not last:
            ybc = _mix_b_ctx_call(zc, b)
            xc = _post_call(xc, yac, ybc, yccf, yccb, zc, mod_c, *weights, min(tm, b * cl), False)
    return x
```
